```python
import jax, jax.numpy as jnp
from jax import lax
import numpy as np

D_MODEL = 4096
BATCH = 4
SEQ = 2048
DEPTH = 2
DEC_BATCH = 128
DEC_SEQ = 1
PAST_LEN = 16384
PAGE_SIZE = 128

N_A_LAYERS = DEPTH // 2
N_B_LAYERS = DEPTH - N_A_LAYERS
EPS = 1e-6

MEM_TOKENS = 256
MEM_HEADS = 4
MEM_HEAD_DIM = D_MODEL // 16
MEM_W = MEM_HEADS * MEM_HEAD_DIM
MIX_W = D_MODEL - MEM_W

CHUNK = 128
A_GROUPS = 8
A_W = MIX_W
A_GROUP_DIM = A_W // A_GROUPS

MLA_HEADS = 24
QK_NOPE = 128
QK_ROPE = 64
QK_HEAD = QK_NOPE + QK_ROPE
V_HEAD = MIX_W // MLA_HEADS
Q_LORA = D_MODEL // 4
KV_LORA = 512
ROPE_THETA = 10000.0
Q_BLOCK = 128

PEER_HEADS = 8
PEER_KEYS = 128
PEER_EXPERTS = PEER_KEYS * PEER_KEYS
PEER_QDIM = 256
PEER_HALF = PEER_QDIM // 2
PEER_TOPK = 16
PEER_BLOCK = 64

kernel_name = 'yoco_gmlp_mla_peer_step'


def rms_norm(x, g):
    xf = x.astype(jnp.float32)
    y = xf * lax.rsqrt(jnp.mean(xf * xf, axis=-1, keepdims=True) + EPS)
    return (y * g.astype(jnp.float32)).astype(x.dtype)


def rope_tables(pos):
    inv = 1.0 / (ROPE_THETA ** (jnp.arange(0, QK_ROPE, 2, dtype=jnp.float32) / QK_ROPE))
    ang = pos.astype(jnp.float32)[:, None] * inv[None, :]
    return jnp.cos(ang), jnp.sin(ang)


def apply_rope(x, cos, sin):
    half = x.shape[-1] // 2
    xf = x.astype(jnp.float32)
    x1, x2 = xf[..., :half], xf[..., half:]
    return jnp.concatenate([x1 * cos - x2 * sin, x2 * cos + x1 * sin], axis=-1).astype(x.dtype)


def chunk_spatial_gating(u, v, w_s, b_s):
    n, L, _ = u.shape
    if L <= CHUNK:
        blk, lp = L, L
    else:
        blk, lp = CHUNK, -(-L // CHUNK) * CHUNK
    v = jnp.pad(v, ((0, 0), (0, lp - L), (0, 0)))
    ws = jnp.tril(w_s[:, :blk, :blk])
    vc = v.reshape(n, lp // blk, blk, A_GROUPS, A_GROUP_DIM)
    mixed = jnp.einsum('gij,ncjgd->ncigd', ws, vc) + b_s[:, :blk].T[None, None, :, :, None]
    return u * mixed.reshape(n, lp, A_W)[:, :L]


def mixer_a(h, w_in, g_v, w_s, b_s):
    z = h @ w_in
    u = jax.nn.gelu(z[..., :A_W], approximate=False)
    v = rms_norm(jax.nn.gelu(z[..., A_W:2 * A_W], approximate=False), g_v)
    return chunk_spatial_gating(u, v, w_s, b_s), z[..., 2 * A_W:], v


def mla_query(h, w_in, g_cq, w_qb, g_q, cos, sin):
    n, L, _ = h.shape
    z = h @ w_in
    q = (rms_norm(z[..., :Q_LORA], g_cq) @ w_qb).reshape(n, L, MLA_HEADS, QK_HEAD)
    q = rms_norm(q, g_q)
    q_rope = apply_rope(q[..., QK_NOPE:], cos[:, None, :], sin[:, None, :])
    return q[..., :QK_NOPE], q_rope, z[..., Q_LORA:]


def shared_latent_kv(x, cos, sin, kv_g_in, kv_w_dkv, kv_w_kr, kv_g_c, kv_g_k, kv_w_uk):
    h = rms_norm(x, kv_g_in)
    c = rms_norm(h @ kv_w_dkv, kv_g_c)
    kr = h @ kv_w_kr
    k_nope = jnp.einsum('nlc,chd->nlhd', c, kv_w_uk)
    kn = k_nope.astype(jnp.float32)
    krf = kr.astype(jnp.float32)
    ms = (jnp.sum(kn * kn, axis=-1) + jnp.sum(krf * krf, axis=-1)[..., None]) / QK_HEAD
    kscale = lax.rsqrt(ms + EPS).astype(x.dtype)
    krope = apply_rope(kr * kv_g_k[QK_NOPE:], cos, sin)
    return c, krope, kscale, k_nope


def mla_prompt_attention(q_nope, q_rope, kv, kv_g_k, kv_w_uv):
    c, krope, kscale, k_nope = kv
    n, L, H, _ = q_nope.shape
    k = jnp.concatenate([k_nope * kv_g_k[:QK_NOPE],
                         jnp.broadcast_to(krope[:, :, None, :], (n, L, H, QK_ROPE))], axis=-1)
    k = k * kscale[..., None]
    q = jnp.concatenate([q_nope, q_rope], axis=-1)
    v = jnp.einsum('nlc,chd->nlhd', c, kv_w_uv)
    blk = min(Q_BLOCK, L)
    nq = L // blk
    qb = q.reshape(n, nq, blk, H, QK_HEAD).transpose(1, 0, 2, 3, 4)
    key_pos = jnp.arange(L)
    scale = QK_HEAD ** -0.5

    def block(args):
        qi, i = args
        s = jnp.einsum('nqhd,nkhd->nhqk', qi, k).astype(jnp.float32) * scale
        q_pos = i * blk + jnp.arange(blk)
        s = jnp.where(key_pos[None, :] <= q_pos[:, None], s, -jnp.inf)
        p = jax.nn.softmax(s, axis=-1).astype(v.dtype)
        return jnp.einsum('nhqk,nkhd->nqhd', p, v)

    o = lax.map(block, (qb, jnp.arange(nq)))
    return o.transpose(1, 0, 2, 3, 4).reshape(n, L, H * V_HEAD)


def mla_sample_attention(q_nope, q_rope, kv, cache_ckv, cache_krope, cache_kscale, page_table,
                         kv_g_k, kv_w_uk, kv_w_uv):
    c_new, krope_new, kscale_new, _ = kv
    n, L, H, _ = q_nope.shape
    scale = QK_HEAD ** -0.5
    q_lat = jnp.einsum('nlhd,chd->nhlc', q_nope * kv_g_k[:QK_NOPE], kv_w_uk).astype(jnp.float32)
    q_r = q_rope.transpose(0, 2, 1, 3).astype(jnp.float32)

    def scores(c_rows, kr_rows, ks_rows):
        s = (jnp.einsum('nhlc,nkc->nhlk', q_lat, c_rows.astype(jnp.float32))
             + jnp.einsum('nhlr,nkr->nhlk', q_r, kr_rows.astype(jnp.float32)))
        return s * ks_rows.astype(jnp.float32).transpose(0, 2, 1)[:, :, None, :] * scale

    s0 = scores(c_new, krope_new, kscale_new)
    s0 = jnp.where(jnp.tril(jnp.ones((L, L), dtype=bool)), s0, -jnp.inf)
    m0 = jnp.max(s0, axis=-1)
    p0 = jnp.exp(s0 - m0[..., None])
    l0 = jnp.sum(p0, axis=-1)
    acc0 = jnp.einsum('nhlk,nkc->nhlc', p0, c_new.astype(jnp.float32))

    def page_step(carry, phys):
        m, l, acc = carry
        c_rows = jnp.take(cache_ckv, phys, axis=0)
        s = scores(c_rows, jnp.take(cache_krope, phys, axis=0), jnp.take(cache_kscale, phys, axis=0))
        m_new = jnp.maximum(m, jnp.max(s, axis=-1))
        alpha = jnp.exp(m - m_new)
        p = jnp.exp(s - m_new[..., None])
        l = l * alpha + jnp.sum(p, axis=-1)
        acc = acc * alpha[..., None] + jnp.einsum('nhlk,nkc->nhlc', p, c_rows.astype(jnp.float32))
        return (m_new, l, acc), None

    (m, l, acc), _ = lax.scan(page_step, (m0, l0, acc0), page_table.T)
    o_lat = (acc / l[..., None]).astype(c_new.dtype)
    o = jnp.einsum('nhlc,chd->nlhd', o_lat, kv_w_uv)
    return o.reshape(n, L, H * V_HEAD)


def memory_kv(mem, g_in, w_k, w_v, g_k):
    n, m, _ = mem.shape
    h = rms_norm(mem, g_in)
    k = rms_norm((h @ w_k).reshape(n, m, MEM_HEADS, MEM_HEAD_DIM), g_k)
    v = (h @ w_v).reshape(n, m, MEM_HEADS, MEM_HEAD_DIM)
    return k, v


def memory_attention(qm, g_q, k, v):
    n, L, _ = qm.shape
    q = rms_norm(qm.reshape(n, L, MEM_HEADS, MEM_HEAD_DIM), g_q)
    s = jnp.einsum('nlhd,nmhd->nhlm', q, k).astype(jnp.float32) * (MEM_HEAD_DIM ** -0.5)
    p = jax.nn.softmax(s, axis=-1).astype(v.dtype)
    return jnp.einsum('nhlm,nmhd->nlhd', p, v).reshape(n, L, MEM_W)


def peer_ffn(h, wq, subkeys, table_u, table_v):
    n, L, D = h.shape
    T = n * L
    t = h.reshape(T, D)
    q = (t @ wq).reshape(T, PEER_HEADS, 2, PEER_HALF)
    s = jnp.einsum('tphd,phkd->tphk', q, subkeys).astype(jnp.float32)
    top_s, top_i = lax.top_k(s, PEER_TOPK)
    cand_s = (top_s[:, :, 0, :, None] + top_s[:, :, 1, None, :]).reshape(T, PEER_HEADS, PEER_TOPK * PEER_TOPK)
    cand_i = (top_i[:, :, 0, :, None] * PEER_KEYS + top_i[:, :, 1, None, :]).reshape(T, PEER_HEADS, PEER_TOPK * PEER_TOPK)
    best_s, best_pos = lax.top_k(cand_s, PEER_TOPK)
    idx = jnp.take_along_axis(cand_i, best_pos, axis=-1)
    gate = jax.nn.softmax(best_s, axis=-1).astype(h.dtype)
    pad = (-T) % PEER_BLOCK
    nb = (T + pad) // PEER_BLOCK
    t_b = jnp.pad(t, ((0, pad), (0, 0))).reshape(nb, PEER_BLOCK, D)
    i_b = jnp.pad(idx, ((0, pad), (0, 0), (0, 0))).reshape(nb, PEER_BLOCK, PEER_HEADS, PEER_TOPK)
    g_b = jnp.pad(gate, ((0, pad), (0, 0), (0, 0))).reshape(nb, PEER_BLOCK, PEER_HEADS, PEER_TOPK)

    def block(args):
        tb, ib, gb = args
        u = jnp.take(table_u, ib, axis=0)
        a = jax.nn.gelu(jnp.einsum('td,tpkd->tpk', tb, u), approximate=False)
        vv = jnp.take(table_v, ib, axis=0)
        return jnp.einsum('tpk,tpkd->td', gb * a, vv)

    out = lax.map(block, (t_b, i_b, g_b))
    return out.reshape(T + pad, D)[:T].reshape(n, L, D)


def setup_inputs(seed: int = 0) -> dict:
    key = jax.random.key(seed)
    ks = list(jax.random.split(key, 48))
    f32 = jnp.float32

    def nrm(shape, scale):
        return jax.random.normal(ks.pop(), shape, f32) * scale

    def gain(shape):
        return 1.0 + 0.02 * jax.random.normal(ks.pop(), shape, f32)

    n_pages = PAST_LEN // PAGE_SIZE
    n_pool = (DEC_BATCH * n_pages * 5) // 4
    page_table = jax.random.permutation(ks.pop(), n_pool)[:DEC_BATCH * n_pages]
    page_table = page_table.reshape(DEC_BATCH, n_pages).astype(jnp.int32)
    d = D_MODEL
    return {
        'x_prompt': nrm((BATCH, SEQ, d), 1.0),
        'x_sample': nrm((DEC_BATCH, DEC_SEQ, d), 1.0),
        'cache_ckv': nrm((n_pool, PAGE_SIZE, KV_LORA), 1.0),
        'cache_krope': nrm((n_pool, PAGE_SIZE, QK_ROPE), 1.0),
        'cache_kscale': jax.random.uniform(ks.pop(), (n_pool, PAGE_SIZE, MLA_HEADS), f32, 0.5, 1.5),
        'cache_mem_k': nrm((DEPTH, DEC_BATCH, MEM_TOKENS, MEM_HEADS, MEM_HEAD_DIM), 1.0),
        'cache_mem_v': nrm((DEPTH, DEC_BATCH, MEM_TOKENS, MEM_HEADS, MEM_HEAD_DIM), 1.0),
        'page_table': page_table,
        'mem_prompt': nrm((BATCH, MEM_TOKENS, d), 1.0),
        'g_mix': gain((DEPTH, d)),
        'g_ffn': gain((DEPTH, d)),
        'g_mem_in': gain((DEPTH, d)),
        'w_mem_k': nrm((DEPTH, d, MEM_W), d ** -0.5),
        'w_mem_v': nrm((DEPTH, d, MEM_W), d ** -0.5),
        'g_mem_q': gain((DEPTH, MEM_HEAD_DIM)),
        'g_mem_k': gain((DEPTH, MEM_HEAD_DIM)),
        'w_out': nrm((DEPTH, MIX_W + MEM_W, d), (MIX_W + MEM_W) ** -0.5),
        'peer_wq': nrm((DEPTH, d, PEER_HEADS * PEER_QDIM), d ** -0.5),
        'peer_subkeys': nrm((DEPTH, PEER_HEADS, 2, PEER_KEYS, PEER_HALF), PEER_HALF ** -0.5),
        'peer_u': nrm((DEPTH, PEER_EXPERTS, d), d ** -0.5),
        'peer_v': nrm((DEPTH, PEER_EXPERTS, d), PEER_HEADS ** -0.5),
        'a_w_in': nrm((N_A_LAYERS, d, 2 * A_W + MEM_W), d ** -0.5),
        'a_g_v': gain((N_A_LAYERS, A_W)),
        'a_w_s': nrm((N_A_LAYERS, A_GROUPS, CHUNK, CHUNK), CHUNK ** -0.5),
        'a_b_s': gain((N_A_LAYERS, A_GROUPS, CHUNK)),
        'b_w_in': nrm((N_B_LAYERS, d, Q_LORA + MEM_W), d ** -0.5),
        'b_g_cq': gain((N_B_LAYERS, Q_LORA)),
        'b_w_qb': nrm((N_B_LAYERS, Q_LORA, MLA_HEADS * QK_HEAD), Q_LORA ** -0.5),
        'b_g_q': gain((N_B_LAYERS, QK_HEAD)),
        'kv_g_in': gain((d,)),
        'kv_w_dkv': nrm((d, KV_LORA), d ** -0.5),
        'kv_w_kr': nrm((d, QK_ROPE), d ** -0.5),
        'kv_g_c': gain((KV_LORA,)),
        'kv_g_k': gain((QK_HEAD,)),
        'kv_w_uk': nrm((KV_LORA, MLA_HEADS, QK_NOPE), KV_LORA ** -0.5),
        'kv_w_uv': nrm((KV_LORA, MLA_HEADS, V_HEAD), KV_LORA ** -0.5),
    }


def reference(x_prompt, x_sample, cache_ckv, cache_krope, cache_kscale, cache_mem_k, cache_mem_v,
              page_table, mem_prompt,
              g_mix, g_ffn, g_mem_in, w_mem_k, w_mem_v, g_mem_q, g_mem_k, w_out,
              peer_wq, peer_subkeys, peer_u, peer_v,
              a_w_in, a_g_v, a_w_s, a_b_s,
              b_w_in, b_g_cq, b_w_qb, b_g_q,
              kv_g_in, kv_w_dkv, kv_w_kr, kv_g_c, kv_g_k, kv_w_uk, kv_w_uv):
    past = page_table.shape[1] * PAGE_SIZE
    cos_p, sin_p = rope_tables(jnp.arange(x_prompt.shape[1]))
    cos_s, sin_s = rope_tables(past + jnp.arange(x_sample.shape[1]))
    x_p, x_s = x_prompt, x_sample
    mem_k_new, mem_v_new, chunk_v_new = [], [], []
    kv_p = None
    kv_s = None
    for l in range(DEPTH):
        mk_p, mv_p = memory_kv(mem_prompt, g_mem_in[l], w_mem_k[l], w_mem_v[l], g_mem_k[l])
        mem_k_new.append(mk_p)
        mem_v_new.append(mv_p)
        if l == N_A_LAYERS:
            kv_p = shared_latent_kv(x_p, cos_p, sin_p, kv_g_in, kv_w_dkv, kv_w_kr, kv_g_c, kv_g_k, kv_w_uk)
            kv_s = shared_latent_kv(x_s, cos_s, sin_s, kv_g_in, kv_w_dkv, kv_w_kr, kv_g_c, kv_g_k, kv_w_uk)
        h_p = rms_norm(x_p, g_mix[l])
        h_s = rms_norm(x_s, g_mix[l])
        if l < N_A_LAYERS:
            mix_p, qm_p, _ = mixer_a(h_p, a_w_in[l], a_g_v[l], a_w_s[l], a_b_s[l])
            mix_s, qm_s, v_s = mixer_a(h_s, a_w_in[l], a_g_v[l], a_w_s[l], a_b_s[l])
            chunk_v_new.append(v_s)
        else:
            b = l - N_A_LAYERS
            qn_p, qr_p, qm_p = mla_query(h_p, b_w_in[b], b_g_cq[b], b_w_qb[b], b_g_q[b], cos_p, sin_p)
            qn_s, qr_s, qm_s = mla_query(h_s, b_w_in[b], b_g_cq[b], b_w_qb[b], b_g_q[b], cos_s, sin_s)
            mix_p = mla_prompt_attention(qn_p, qr_p, kv_p, kv_g_k, kv_w_uv)
            mix_s = mla_sample_attention(qn_s, qr_s, kv_s, cache_ckv, cache_krope, cache_kscale,
                                         page_table, kv_g_k, kv_w_uk, kv_w_uv)
        mo_p = memory_attention(qm_p, g_mem_q[l], mk_p, mv_p)
        mo_s = memory_attention(qm_s, g_mem_q[l], cache_mem_k[l], cache_mem_v[l])
        x_p = x_p + jnp.concatenate([mix_p, mo_p], axis=-1) @ w_out[l]
        x_s = x_s + jnp.concatenate([mix_s, mo_s], axis=-1) @ w_out[l]
        x_p = x_p + peer_ffn(rms_norm(x_p, g_ffn[l]), peer_wq[l], peer_subkeys[l], peer_u[l], peer_v[l])
        x_s = x_s + peer_ffn(rms_norm(x_s, g_ffn[l]), peer_wq[l], peer_subkeys[l], peer_u[l], peer_v[l])
    return (x_p, x_s, kv_p[0], kv_p[1], kv_p[2], jnp.stack(mem_k_new), jnp.stack(mem_v_new),
            kv_s[0], kv_s[1], kv_s[2], jnp.stack(chunk_v_new))
```

```python
import functools
import math

import jax
import jax.numpy as jnp
from jax import lax
from jax.experimental import pallas as pl
from jax.experimental.pallas import tpu as pltpu

EPS = 1e-6
ROPE_THETA = 10000.0
_BF = jnp.bfloat16
_F32 = jnp.float32
_MIB = 1024 * 1024
_SQRT_HALF = 0.7071067811865476
_NT = (((1,), (1,)), ((), ()))


def _params(semantics, vmem_mib):
    return pltpu.CompilerParams(dimension_semantics=semantics, vmem_limit_bytes=vmem_mib * _MIB)


def _gelu(x):
    return 0.5 * x * (1.0 + lax.erf(x * _SQRT_HALF))


def _rms(x, g):
    return x * lax.rsqrt(jnp.mean(x * x, axis=-1, keepdims=True) + EPS) * g


def _rmsnorm_body(x_ref, g_ref, o_ref):
    o_ref[...] = _rms(x_ref[...], g_ref[...]).astype(o_ref.dtype)


def rmsnorm(x, g, tm):
    m, d = x.shape
    return pl.pallas_call(
        _rmsnorm_body,
        grid=(m // tm,),
        in_specs=[pl.BlockSpec((tm, d), lambda i: (i, 0)), pl.BlockSpec((1, d), lambda i: (0, 0))],
        out_specs=pl.BlockSpec((tm, d), lambda i: (i, 0)),
        out_shape=jax.ShapeDtypeStruct((m, d), _BF),
        compiler_params=_params(("parallel",), 40),
        name="rmsnorm",
    )(x, g.reshape(1, d))


def _mm_body(*refs, n_lhs, has_res, n_extra, epilogue):
    o_ref = refs[-1]
    acc = None
    for a_ref, w_ref in zip(refs[:n_lhs], refs[n_lhs:2 * n_lhs]):
        d = jnp.dot(a_ref[...], w_ref[...], preferred_element_type=_F32)
        acc = d if acc is None else acc + d
    pos = 2 * n_lhs
    if has_res:
        acc = acc + refs[pos][...]
        pos += 1
    if epilogue is not None:
        acc = epilogue(acc, *[r[...] for r in refs[pos:pos + n_extra]])
    o_ref[...] = acc.astype(o_ref.dtype)


def matmul(lhs, rhs, *, tm, tn, out_dtype, res=None, extras=(), epilogue=None, name="matmul"):
    m = lhs[0].shape[0]
    n = rhs[0].shape[1]
    in_specs = [pl.BlockSpec((tm, a.shape[1]), lambda i, j: (i, 0)) for a in lhs]
    in_specs += [pl.BlockSpec((w.shape[0], tn), lambda i, j: (0, j)) for w in rhs]
    args = list(lhs) + list(rhs)
    if res is not None:
        in_specs.append(pl.BlockSpec((tm, tn), lambda i, j: (i, j)))
        args.append(res)
    for e in extras:
        if e.shape[1] == n:
            in_specs.append(pl.BlockSpec((1, tn), lambda i, j: (0, j)))
        else:
            in_specs.append(pl.BlockSpec(e.shape, lambda i, j: (0, 0)))
        args.append(e)
    body = functools.partial(_mm_body, n_lhs=len(lhs), has_res=res is not None,
                             n_extra=len(extras), epilogue=epilogue)
    return pl.pallas_call(
        body,
        grid=(m // tm, n // tn),
        in_specs=in_specs,
        out_specs=pl.BlockSpec((tm, tn), lambda i, j: (i, j)),
        out_shape=jax.ShapeDtypeStruct((m, n), out_dtype),
        compiler_params=_params(("parallel", "arbitrary"), 48),
        name=name,
    )(*args)


def _headnorm_epilogue(acc, g, *, width):
    parts = [_rms(acc[:, c:c + width], g) for c in range(0, acc.shape[1], width)]
    return jnp.concatenate(parts, axis=-1)


def _gating_body(zu_ref, zv_ref, gv_ref, ws_ref, b_ref, mix_ref, v_ref, *, groups):
    u = _gelu(zu_ref[...])
    v = _rms(_gelu(zv_ref[...]), gv_ref[...])

    @pl.when(pl.program_id(0) == pl.num_programs(0) - 1)
    def _():
        v_ref[...] = v

    gd = v.shape[1] // groups
    b = b_ref[0]
    for g in range(groups):
        sl = slice(g * gd, (g + 1) * gd)
        mixed = jnp.dot(ws_ref[0, g].astype(_BF), v[:, sl].astype(_BF), preferred_element_type=_F32)
        mix_ref[:, sl] = (u[:, sl] * (mixed + b[:, g:g + 1])).astype(mix_ref.dtype)


def gating(z, g_v, ws_eff, b_eff, a_w, n_prompt_chunks):
    t = z.shape[0]
    _, groups, chunk, _ = ws_eff.shape
    n_chunks = t // chunk
    return pl.pallas_call(
        functools.partial(_gating_body, groups=groups),
        grid=(n_chunks,),
        in_specs=[
            pl.BlockSpec((chunk, a_w), lambda i: (i, 0)),
            pl.BlockSpec((chunk, a_w), lambda i: (i, 1)),
            pl.BlockSpec((1, a_w), lambda i: (0, 0)),
            pl.BlockSpec((1, groups, chunk, chunk), lambda i: (i // n_prompt_chunks, 0, 0, 0)),
            pl.BlockSpec((1, chunk, groups), lambda i: (i // n_prompt_chunks, 0, 0)),
        ],
        out_specs=[
            pl.BlockSpec((chunk, a_w), lambda i: (i, 0)),
            pl.BlockSpec((chunk, a_w), lambda i: (0, 0)),
        ],
        out_shape=[jax.ShapeDtypeStruct((t, a_w), _BF), jax.ShapeDtypeStruct((chunk, a_w), _F32)],
        compiler_params=_params(("arbitrary",), 40),
        name="gating",
    )(z, z, g_v.reshape(1, a_w), ws_eff, b_eff)


def _memattn_prompt_body(q_ref, k_ref, v_ref, g_ref, o_ref, *, heads):
    hd = q_ref.shape[1] // heads
    scale = hd ** -0.5
    for h in range(heads):
        sl = slice(h * hd, (h + 1) * hd)
        q = _rms(q_ref[:, sl], g_ref[...]).astype(_BF)
        s = lax.dot_general(q, k_ref[0, :, sl].astype(_BF), _NT, preferred_element_type=_F32) * scale
        e = jnp.exp(s - jnp.max(s, axis=-1, keepdims=True))
        p = (e / jnp.sum(e, axis=-1, keepdims=True)).astype(_BF)
        o_ref[:, sl] = jnp.dot(p, v_ref[0, :, sl].astype(_BF), preferred_element_type=_F32).astype(o_ref.dtype)


def memattn_prompt(z, col_block, mem_k, mem_v, g_q, *, n_batch, seq, tq, heads):
    mem_w = mem_k.shape[2]
    mt = mem_k.shape[1]
    nq = seq // tq
    return pl.pallas_call(
        functools.partial(_memattn_prompt_body, heads=heads),
        grid=(n_batch, nq),
        in_specs=[
            pl.BlockSpec((tq, mem_w), lambda n, i: (n * nq + i, col_block)),
            pl.BlockSpec((1, mt, mem_w), lambda n, i: (n, 0, 0)),
            pl.BlockSpec((1, mt, mem_w), lambda n, i: (n, 0, 0)),
            pl.BlockSpec((1, mem_w // heads), lambda n, i: (0, 0)),
        ],
        out_specs=pl.BlockSpec((tq, mem_w), lambda n, i: (n * nq + i, 0)),
        out_shape=jax.ShapeDtypeStruct((n_batch * seq, mem_w), _BF),
        compiler_params=_params(("parallel", "arbitrary"), 40),
        name="memattn_prompt",
    )(z, mem_k, mem_v, g_q.reshape(1, -1))


def _memattn_sample_body(q_ref, k_ref, v_ref, g_ref, o_ref, *, heads):
    hd = q_ref.shape[2] // heads
    scale = hd ** -0.5
    for h in range(heads):
        sl = slice(h * hd, (h + 1) * hd)
        q = _rms(q_ref[0, :, sl], g_ref[...])
        q8 = jnp.broadcast_to(q, (8, hd)).astype(_BF)
        s = lax.dot_general(q8, k_ref[0, 0, :, h, :].astype(_BF), _NT, preferred_element_type=_F32) * scale
        e = jnp.exp(s - jnp.max(s, axis=-1, keepdims=True))
        p = (e / jnp.sum(e, axis=-1, keepdims=True)).astype(_BF)
        o = jnp.dot(p, v_ref[0, 0, :, h, :].astype(_BF), preferred_element_type=_F32)
        o_ref[0, :, sl] = o[0:1].astype(o_ref.dtype)


def memattn_sample(qm, cache_k, cache_v, layer, g_q):
    _, n, mt, heads, hd = cache_k.shape
    mem_w = heads * hd
    return pl.pallas_call(
        functools.partial(_memattn_sample_body, heads=heads),
        grid=(n,),
        in_specs=[
            pl.BlockSpec((1, 1, mem_w), lambda i: (i, 0, 0)),
            pl.BlockSpec((1, 1, mt, heads, hd), lambda i: (layer, i, 0, 0, 0)),
            pl.BlockSpec((1, 1, mt, heads, hd), lambda i: (layer, i, 0, 0, 0)),
            pl.BlockSpec((1, hd), lambda i: (0, 0)),
        ],
        out_specs=pl.BlockSpec((1, 1, mem_w), lambda i: (i, 0, 0)),
        out_shape=jax.ShapeDtypeStruct((n, 1, mem_w), _BF),
        compiler_params=_params(("parallel",), 40),
        name="memattn_sample",
    )(qm, cache_k, cache_v, g_q.reshape(1, hd))


def _topk16(x, on_pick):
    rows = x.shape[0]
    iota = lax.broadcasted_iota(jnp.int32, x.shape, 0).astype(_F32)
    for k in range(16):
        m = jnp.max(x, axis=0, keepdims=True)
        pos = jnp.min(jnp.where(x == m, iota, float(rows)), axis=0, keepdims=True)
        hit = iota == pos
        on_pick(k, m, hit, pos)
        x = jnp.where(hit, -jnp.inf, x)


def _route_body(q_ref, sk_ref, h1_ref, e1_ref, r2_ref, e2_ref):
    tb = q_ref.shape[0]
    nk, half = sk_ref.shape[1:]
    big = float(nk)
    row16 = lax.broadcasted_iota(jnp.int32, (16, tb), 0).astype(_F32)
    scores, ranks, tops = [], [], []
    for h in range(2):
        s = lax.dot_general(sk_ref[h], q_ref[:, h * half:(h + 1) * half], _NT, preferred_element_type=_F32)
        state = {"rank": jnp.full((nk, tb), big, _F32), "vals": []}

        def pick(k, m, hit, pos, state=state):
            state["rank"] = jnp.where(hit, float(k), state["rank"])
            state["vals"].append(m)

        _topk16(s, pick)
        scores.append(s)
        ranks.append(state["rank"])
        tops.append(state["vals"])
    v2 = jnp.zeros((16, tb), _F32)
    for k in range(16):
        v2 = jnp.where(row16 == float(k), tops[1][k], v2)
    cand = jnp.concatenate([tops[0][a] + v2 for a in range(16)], axis=0)
    best0 = tops[0][0] + tops[1][0]
    st = {"cnt": jnp.zeros((16, tb), _F32), "z": jnp.zeros((1, tb), _F32)}

    def pick2(k, m, hit, pos):
        a = jnp.floor(pos * 0.0625)
        st["cnt"] = st["cnt"] + jnp.where(row16 == a, 1.0, 0.0)
        st["z"] = st["z"] + jnp.exp(m - best0)

    _topk16(cand, pick2)
    h1 = jnp.zeros((nk, tb), _F32)
    for a in range(16):
        h1 = h1 + jnp.where(ranks[0] == float(a), st["cnt"][a:a + 1, :], 0.0)
    h1_ref[0] = h1
    r2_ref[0] = ranks[1]
    e1_ref[0] = jnp.exp(scores[0] - tops[0][0])
    e2_ref[0] = jnp.exp(scores[1] - tops[1][0]) / st["z"]


def peer_route(q, subkeys, tb):
    t = q.shape[0]
    n2, nk, half = subkeys.shape
    heads = n2 // 2
    out = jax.ShapeDtypeStruct((heads, nk, t), _F32)
    ospec = pl.BlockSpec((1, nk, tb), lambda i, p: (p, 0, i))
    return pl.pallas_call(
        _route_body,
        grid=(t // tb, heads),
        in_specs=[
            pl.BlockSpec((tb, 2 * half), lambda i, p: (i, p)),
            pl.BlockSpec((2, nk, half), lambda i, p: (p, 0, 0)),
        ],
        out_specs=[ospec, ospec, ospec, ospec],
        out_shape=[out, out, out, out],
        compiler_params=_params(("parallel", "arbitrary"), 40),
        name="peer_route",
    )(q, subkeys)


def _peer_body(hf_ref, u_ref, vt_ref, h1_ref, e1_ref, r2_ref, e2_ref, *rest, heads, nk):
    o_ref = rest[-1]
    j = pl.program_id(1)

    @pl.when(j == 0)
    def _():
        o_ref[...] = jnp.zeros_like(o_ref)

    te = u_ref.shape[0]
    at = lax.dot_general(u_ref[...], hf_ref[...], _NT, preferred_element_type=_F32)
    pieces = []
    for c in range(te // nk):
        i1 = j * (te // nk) + c
        g = None
        for p in range(heads):
            h1row = h1_ref[p, pl.ds(i1, 1), :]
            e1row = e1_ref[p, pl.ds(i1, 1), :]
            term = jnp.where(r2_ref[p] < h1row, e1row * e2_ref[p], 0.0)
            g = term if g is None else g + term
        pieces.append((_gelu(at[c * nk:(c + 1) * nk, :]) * g).astype(_BF))
    bt = jnp.concatenate(pieces, axis=0)
    o_ref[...] += jnp.dot(vt_ref[...], bt, preferred_element_type=_F32)


def peer_dense(hf, u, vt, route, *, tm, te, row_block0, n_row_blocks, prev=None):
    t, d = hf.shape
    n_exp = u.shape[0]
    heads, nk, _ = route[0].shape
    rspec = pl.BlockSpec((heads, nk, tm), lambda i, j: (0, 0, row_block0 + i))
    in_specs = [
        pl.BlockSpec((tm, d), lambda i, j: (row_block0 + i, 0)),
        pl.BlockSpec((te, d), lambda i, j: (j, 0)),
        pl.BlockSpec((d, te), lambda i, j: (0, j)),
        rspec, rspec, rspec, rspec,
    ]
    args = [hf, u, vt, *route]
    aliases = {}
    if prev is not None:
        in_specs.append(pl.BlockSpec(memory_space=pl.ANY))
        args.append(prev)
        aliases = {len(args) - 1: 0}
    return pl.pallas_call(
        functools.partial(_peer_body, heads=heads, nk=nk),
        grid=(n_row_blocks, n_exp // te),
        in_specs=in_specs,
        out_specs=pl.BlockSpec((d, tm), lambda i, j: (0, row_block0 + i)),
        out_shape=jax.ShapeDtypeStruct((d, t), _F32),
        input_output_aliases=aliases,
        compiler_params=_params(("parallel", "arbitrary"), 56),
        name="peer_dense",
    )(*args)


def _add_transposed_body(x_ref, yt_ref, o_ref):
    o_ref[...] = x_ref[...] + yt_ref[...].T


def add_transposed(x, yt, tb):
    t, d = x.shape
    return pl.pallas_call(
        _add_transposed_body,
        grid=(t // tb,),
        in_specs=[pl.BlockSpec((tb, d), lambda i: (i, 0)), pl.BlockSpec((d, tb), lambda i: (0, i))],
        out_specs=pl.BlockSpec((tb, d), lambda i: (i, 0)),
        out_shape=jax.ShapeDtypeStruct((t, d), _F32),
        compiler_params=_params(("parallel",), 40),
        name="add_transposed",
    )(x, yt)


def peer_ffn(x, g_ffn, wq, subkeys, u, v, *, n_prompt):
    t, d = x.shape
    heads, _, nk, half = subkeys.shape
    hf = rmsnorm(x, g_ffn, 416)
    q = matmul([hf], [wq.astype(_BF)], tm=832, tn=512, out_dtype=_BF, name="peer_q")
    route = peer_route(q, subkeys.reshape(heads * 2, nk, half).astype(_BF), 128)
    u_b = u.astype(_BF)
    vt_b = v.T.astype(_BF)
    tm = 512
    yt = peer_dense(hf, u_b, vt_b, route, tm=tm, te=256, row_block0=0, n_row_blocks=n_prompt // tm)
    ts = t - n_prompt
    yt = peer_dense(hf, u_b, vt_b, route, tm=ts, te=256, row_block0=n_prompt // ts, n_row_blocks=1, prev=yt)
    return add_transposed(x, yt, 128)


def _kv_body(ckr_ref, cs_ref, gc_ref, gkn_ref, gkr_ref, wuk_ref, wuv_ref,
             c_ref, krope_ref, ks_ref, kn_ref, v_ref, *, heads, qk_head):
    kv_lora = c_ref.shape[1]
    c = _rms(ckr_ref[:, :kv_lora], gc_ref[...])
    c_ref[...] = c
    cb = c.astype(_BF)
    blk = ckr_ref[:, kv_lora:]
    rope = blk.shape[1] // 2
    lane = lax.broadcasted_iota(jnp.int32, blk.shape, 1)
    ssq_r = jnp.sum(jnp.where(lane < rope, blk * blk, 0.0), axis=-1, keepdims=True)
    t = blk * gkr_ref[...] * cs_ref[...]
    krope_ref[...] = t[:, :rope] + t[:, rope:]
    kn = jnp.dot(cb, wuk_ref[...], preferred_element_type=_F32)
    nope = kn.shape[1] // heads
    lane_h = lax.broadcasted_iota(jnp.int32, (kn.shape[0], 128), 1)
    ks = jnp.zeros((kn.shape[0], 128), _F32)
    for h in range(heads):
        x = kn[:, h * nope:(h + 1) * nope]
        ms = (jnp.sum(x * x, axis=-1, keepdims=True) + ssq_r) / qk_head
        ks = jnp.where(lane_h == h, lax.rsqrt(ms + EPS), ks)
        kn_ref[:, h * nope:(h + 1) * nope] = (x * gkn_ref[...]).astype(kn_ref.dtype)
    ks_ref[...] = ks[:, :heads]
    v_ref[...] = jnp.dot(cb, wuv_ref[...], preferred_element_type=_F32).astype(v_ref.dtype)


def latent_kv(ckr, cs_tab, g_c, g_k, w_uk, w_uv, *, heads, tb):
    t, w = ckr.shape
    kv_lora = w_uk.shape[0]
    rope = (w - kv_lora) // 2
    nope = w_uk.shape[1] // heads
    g_kr = g_k[nope:]
    gkr2 = jnp.concatenate([g_kr, jnp.concatenate([g_kr[rope // 2:], g_kr[:rope // 2]])]).reshape(1, 2 * rope)
    full = lambda i: (0, 0)
    row = lambda i: (i, 0)
    return pl.pallas_call(
        functools.partial(_kv_body, heads=heads, qk_head=float(nope + rope)),
        grid=(t // tb,),
        in_specs=[
            pl.BlockSpec((tb, w), row),
            pl.BlockSpec((tb, 2 * rope), row),
            pl.BlockSpec((1, kv_lora), full),
            pl.BlockSpec((1, nope), full),
            pl.BlockSpec((1, 2 * rope), full),
            pl.BlockSpec(w_uk.shape, full),
            pl.BlockSpec(w_uv.shape, full),
        ],
        out_specs=[
            pl.BlockSpec((tb, kv_lora), row),
            pl.BlockSpec((tb, rope), row),
            pl.BlockSpec((tb, heads), row),
            pl.BlockSpec((tb, w_uk.shape[1]), row),
            pl.BlockSpec((tb, w_uv.shape[1]), row),
        ],
        out_shape=[
            jax.ShapeDtypeStruct((t, kv_lora), _F32),
            jax.ShapeDtypeStruct((t, rope), _F32),
            jax.ShapeDtypeStruct((t, heads), _F32),
            jax.ShapeDtypeStruct((t, w_uk.shape[1]), _BF),
            jax.ShapeDtypeStruct((t, w_uv.shape[1]), _BF),
        ],
        compiler_params=_params(("parallel",), 48),
        name="latent_kv",
    )(ckr, cs_tab, g_c.reshape(1, -1), g_k[:nope].reshape(1, -1), gkr2, w_uk, w_uv)


def _q_body(q_ref, c2_ref, s2_ref, gn_ref, gr_ref, grs_ref, qn_ref, qr_ref, *, heads, nope, rope):
    n_all = heads * nope
    r_all = heads * rope
    qk_head = float(nope + rope)
    lane = lax.broadcasted_iota(jnp.int32, (q_ref.shape[0], 2 * rope), 1)
    first = lane < rope
    for j in range(heads // 2):
        n0 = q_ref[:, (2 * j) * nope:(2 * j + 1) * nope]
        n1 = q_ref[:, (2 * j + 1) * nope:(2 * j + 2) * nope]
        r = q_ref[:, n_all + j * 2 * rope:n_all + (j + 1) * 2 * rope]
        rs = q_ref[:, n_all + r_all + j * 2 * rope:n_all + r_all + (j + 1) * 2 * rope]
        r2 = r * r
        ss0 = jnp.sum(n0 * n0, axis=-1, keepdims=True) + jnp.sum(jnp.where(first, r2, 0.0), axis=-1, keepdims=True)
        ss1 = jnp.sum(n1 * n1, axis=-1, keepdims=True) + jnp.sum(jnp.where(first, 0.0, r2), axis=-1, keepdims=True)
        inv0 = lax.rsqrt(ss0 / qk_head + EPS)
        inv1 = lax.rsqrt(ss1 / qk_head + EPS)
        qn_ref[:, (2 * j) * nope:(2 * j + 1) * nope] = n0 * inv0 * gn_ref[...]
        qn_ref[:, (2 * j + 1) * nope:(2 * j + 2) * nope] = n1 * inv1 * gn_ref[...]
        inv_r = jnp.where(first, inv0, inv1)
        qr_ref[:, j * 2 * rope:(j + 1) * 2 * rope] = (
            r * inv_r * gr_ref[...] * c2_ref[...] + rs * inv_r * grs_ref[...] * s2_ref[...])


def mla_q_finish(qraw, c2_tab, s2_tab, g_q, *, heads, nope, rope, tb):
    t, w = qraw.shape
    g_r = g_q[nope:]
    g_rs = jnp.concatenate([g_r[rope // 2:], g_r[:rope // 2]])
    row = lambda i: (i, 0)
    full = lambda i: (0, 0)
    return pl.pallas_call(
        functools.partial(_q_body, heads=heads, nope=nope, rope=rope),
        grid=(t // tb,),
        in_specs=[
            pl.BlockSpec((tb, w), row),
            pl.BlockSpec((tb, 2 * rope), row),
            pl.BlockSpec((tb, 2 * rope), row),
            pl.BlockSpec((1, nope), full),
            pl.BlockSpec((1, 2 * rope), full),
            pl.BlockSpec((1, 2 * rope), full),
        ],
        out_specs=[pl.BlockSpec((tb, heads * nope), row), pl.BlockSpec((tb, heads * rope), row)],
        out_shape=[jax.ShapeDtypeStruct((t, heads * nope), _F32), jax.ShapeDtypeStruct((t, heads * rope), _F32)],
        compiler_params=_params(("parallel",), 48),
        name="mla_q_finish",
    )(qraw, c2_tab, s2_tab, g_q[:nope].reshape(1, -1),
      jnp.tile(g_r, 2).reshape(1, -1), jnp.tile(g_rs, 2).reshape(1, -1))


def _flash_body(qn_ref, qr_ref, kn_ref, kr_ref, ks_ref, v_ref, o_ref, *, nope, rope, tk):
    tq = qn_ref.shape[0]
    qi = pl.program_id(2)
    krope = kr_ref
    q_pos = qi * tq + lax.broadcasted_iota(jnp.int32, (tq, tk), 0)
    k_off = lax.broadcasted_iota(jnp.int32, (tq, tk), 1)
    for hh in range(2):
        qn = qn_ref[:, hh * nope:(hh + 1) * nope].astype(_BF)
        qr = qr_ref[:, hh * rope:(hh + 1) * rope].astype(_BF)

        def step(j, carry, hh=hh, qn=qn, qr=qr):
            m, l, acc = carry
            rows = pl.ds(pl.multiple_of(j * tk, tk), tk)
            s = lax.dot_general(qn, kn_ref[rows, hh * nope:(hh + 1) * nope], _NT, preferred_element_type=_F32)
            s = s + lax.dot_general(qr, krope[rows, :], _NT, preferred_element_type=_F32)
            s = s * ks_ref[0, hh:hh + 1, pl.ds(pl.multiple_of(j * tk, tk), tk)]
            s = jnp.where(j * tk + k_off <= q_pos, s, -jnp.inf)
            m_new = jnp.maximum(m, jnp.max(s, axis=-1, keepdims=True))
            alpha = jnp.exp(m - m_new)
            p = jnp.exp(s - m_new)
            l = l * alpha + jnp.sum(p, axis=-1, keepdims=True)
            acc = acc * alpha + jnp.dot(p.astype(_BF), v_ref[rows, hh * nope:(hh + 1) * nope],
                                        preferred_element_type=_F32)
            return m_new, l, acc

        init = (jnp.full((tq, 1), -jnp.inf, _F32), jnp.zeros((tq, 1), _F32), jnp.zeros((tq, nope), _F32))
        m, l, acc = lax.fori_loop(0, (qi * tq) // tk + tq // tk, step, init)
        o_ref[:, hh * nope:(hh + 1) * nope] = (acc / l).astype(o_ref.dtype)


def mla_prompt_attention(qn, qr, kn, krope, ks_t, v, *, n_batch, seq, heads, nope, rope, tq, tk):
    nq = seq // tq
    hp = heads // 2
    return pl.pallas_call(
        functools.partial(_flash_body, nope=nope, rope=rope, tk=tk),
        grid=(n_batch, hp, nq),
        in_specs=[
            pl.BlockSpec((tq, 2 * nope), lambda n, h, i: (n * nq + i, h)),
            pl.BlockSpec((tq, 2 * rope), lambda n, h, i: (n * nq + i, h)),
            pl.BlockSpec((seq, 2 * nope), lambda n, h, i: (n, h)),
            pl.BlockSpec((seq, rope), lambda n, h, i: (n, 0)),
            pl.BlockSpec((1, 2, seq), lambda n, h, i: (n * hp + h, 0, 0)),
            pl.BlockSpec((seq, 2 * nope), lambda n, h, i: (n, h)),
        ],
        out_specs=pl.BlockSpec((tq, 2 * nope), lambda n, h, i: (n * nq + i, h)),
        out_shape=jax.ShapeDtypeStruct((n_batch * seq, heads * nope), _BF),
        compiler_params=_params(("parallel", "parallel", "arbitrary"), 40),
        name="mla_prompt_attention",
    )(qn, qr, kn, krope, ks_t, v)


def _head_nt_body(x_ref, g_ref, w_ref, o_ref):
    x = (x_ref[...] * g_ref[...]).astype(_BF)
    o_ref[0] = lax.dot_general(x, w_ref[...], _NT, preferred_element_type=_F32)


def absorb_q(qn_s, g_kn, w_uk):
    n = qn_s.shape[0]
    nope = g_kn.shape[0]
    kv_lora, w = w_uk.shape
    heads = w // nope
    return pl.pallas_call(
        _head_nt_body,
        grid=(heads,),
        in_specs=[
            pl.BlockSpec((n, nope), lambda h: (0, h)),
            pl.BlockSpec((1, nope), lambda h: (0, 0)),
            pl.BlockSpec((kv_lora, nope), lambda h: (0, h)),
        ],
        out_specs=pl.BlockSpec((1, n, kv_lora), lambda h: (h, 0, 0)),
        out_shape=jax.ShapeDtypeStruct((heads, n, kv_lora), _F32),
        compiler_params=_params(("parallel",), 40),
        name="absorb_q",
    )(qn_s, g_kn.reshape(1, nope), w_uk)


def _head_nn_body(x_ref, w_ref, o_ref):
    o_ref[...] = jnp.dot(x_ref[0].astype(_BF), w_ref[...], preferred_element_type=_F32).astype(o_ref.dtype)


def project_latent_out(o_lat, w_uv):
    heads, n, kv_lora = o_lat.shape
    vh = w_uv.shape[1] // heads
    return pl.pallas_call(
        _head_nn_body,
        grid=(heads,),
        in_specs=[
            pl.BlockSpec((1, n, kv_lora), lambda h: (h, 0, 0)),
            pl.BlockSpec((kv_lora, vh), lambda h: (0, h)),
        ],
        out_specs=pl.BlockSpec((n, vh), lambda h: (0, h)),
        out_shape=jax.ShapeDtypeStruct((n, heads * vh), _BF),
        compiler_params=_params(("parallel",), 40),
        name="project_latent_out",
    )(o_lat, w_uv)


def _paged_latent_attention_jnp(q_lat, q_r, c_new, kr_new, ks_new, cache_ckv, cache_krope, cache_kscale,
                                page_table, scale):
    def scores(c_rows, kr_rows, ks_rows):
        s = (jnp.einsum('nhc,nkc->nhk', q_lat, c_rows) + jnp.einsum('nhr,nkr->nhk', q_r, kr_rows))
        return s * ks_rows.transpose(0, 2, 1) * scale

    s0 = scores(c_new[:, None, :], kr_new[:, None, :], ks_new[:, None, :])
    m0 = jnp.max(s0, axis=-1)
    p0 = jnp.exp(s0 - m0[..., None])
    l0 = jnp.sum(p0, axis=-1)
    acc0 = jnp.einsum('nhk,nkc->nhc', p0, c_new[:, None, :])

    def page_step(carry, phys):
        m, l, acc = carry
        c_rows = jnp.take(cache_ckv, phys, axis=0)
        s = scores(c_rows, jnp.take(cache_krope, phys, axis=0), jnp.take(cache_kscale, phys, axis=0))
        m_new = jnp.maximum(m, jnp.max(s, axis=-1))
        alpha = jnp.exp(m - m_new)
        p = jnp.exp(s - m_new[..., None])
        l = l * alpha + jnp.sum(p, axis=-1)
        acc = acc * alpha[..., None] + jnp.einsum('nhk,nkc->nhc', p, c_rows)
        return (m_new, l, acc), None

    (m, l, acc), _ = lax.scan(page_step, (m0, l0, acc0), page_table.T)
    return acc / l[..., None]


def _rope_tables(pos, rope):
    inv = 1.0 / (ROPE_THETA ** (jnp.arange(0, rope, 2, dtype=_F32) / rope))
    ang = pos.astype(_F32)[:, None] * inv[None, :]
    return jnp.cos(ang), jnp.sin(ang)


def kernel(x_prompt, x_sample, cache_ckv, cache_krope, cache_kscale, cache_mem_k, cache_mem_v, page_table, mem_prompt, g_mix, g_ffn, g_mem_in, w_mem_k, w_mem_v, g_mem_q, g_mem_k, w_out, peer_wq, peer_subkeys, peer_u, peer_v, a_w_in, a_g_v, a_w_s, a_b_s, b_w_in, b_g_cq, b_w_qb, b_g_q, kv_g_in, kv_w_dkv, kv_w_kr, kv_g_c, kv_g_k, kv_w_uk, kv_w_uv):
    n_batch, seq, d = x_prompt.shape
    n_dec = x_sample.shape[0]
    assert x_sample.shape[1] == 1
    depth = g_mix.shape[0]
    n_a = a_w_in.shape[0]
    mem_tokens, mem_heads, mem_hd = cache_mem_k.shape[2:]
    mem_w = mem_heads * mem_hd
    a_w = a_g_v.shape[1]
    groups, chunk = a_w_s.shape[1], a_w_s.shape[2]
    kv_lora, heads, nope = kv_w_uk.shape
    rope = kv_w_kr.shape[1]
    q_lora = b_g_cq.shape[1]
    n_p = n_batch * seq
    t = n_p + n_dec
    past = page_table.shape[1] * cache_ckv.shape[1]

    x = jnp.concatenate([x_prompt.reshape(n_p, d), x_sample.reshape(n_dec, d)], axis=0)
    mem = mem_prompt.reshape(n_batch * mem_tokens, d)

    cos_p, sin_p = _rope_tables(jnp.arange(seq), rope)
    cos_s, sin_s = _rope_tables(past + jnp.arange(1), rope)
    cos = jnp.concatenate([jnp.tile(cos_p, (n_batch, 1)), jnp.tile(cos_s, (n_dec, 1))], axis=0)
    sin = jnp.concatenate([jnp.tile(sin_p, (n_batch, 1)), jnp.tile(sin_s, (n_dec, 1))], axis=0)
    cos2 = jnp.concatenate([cos, cos], axis=1)
    sin2 = jnp.concatenate([-sin, sin], axis=1)

    mem_k_new, mem_v_new, chunk_v_new = [], [], []
    kv_out = None
    for l in range(depth):
        hm = rmsnorm(mem, g_mem_in[l], 256)
        mk = matmul([hm], [w_mem_k[l].astype(_BF)], tm=512, tn=512, out_dtype=_F32,
                    extras=[g_mem_k[l].reshape(1, mem_hd)],
                    epilogue=functools.partial(_headnorm_epilogue, width=mem_hd), name="mem_k")
        mv = matmul([hm], [w_mem_v[l].astype(_BF)], tm=512, tn=512, out_dtype=_F32, name="mem_v")
        mem_k_new.append(mk.reshape(n_batch, mem_tokens, mem_heads, mem_hd))
        mem_v_new.append(mv.reshape(n_batch, mem_tokens, mem_heads, mem_hd))

        if l == n_a:
            hk = rmsnorm(x, kv_g_in, 416)
            kr_sw = jnp.concatenate([kv_w_kr[:, rope // 2:], kv_w_kr[:, :rope // 2]], axis=1)
            w_ckr = jnp.concatenate([kv_w_dkv, kv_w_kr, kr_sw], axis=1).astype(_BF)
            ckr = matmul([hk], [w_ckr], tm=832, tn=w_ckr.shape[1], out_dtype=_F32, name="kv_down")
            w_uk2 = kv_w_uk.reshape(kv_lora, heads * nope).astype(_BF)
            w_uv2 = kv_w_uv.reshape(kv_lora, -1).astype(_BF)
            c_all, krope_all, ks_all, kn_all, v_all = latent_kv(
                ckr, jnp.concatenate([cos2, sin2], axis=1), kv_g_c, kv_g_k, w_uk2, w_uv2, heads=heads, tb=416)
            kv_out = (c_all, krope_all, ks_all)

        h = rmsnorm(x, g_mix[l], 416)
        if l < n_a:
            z = matmul([h], [a_w_in[l].astype(_BF)], tm=832, tn=512, out_dtype=_F32, name="a_in")
            tril = jnp.tril(a_w_s[l])
            diag = a_w_s[l][:, 0, 0][:, None, None] * jnp.eye(chunk, dtype=_F32)[None]
            ws_eff = jnp.stack([tril, diag])
            b_eff = jnp.stack([a_b_s[l].T, jnp.broadcast_to(a_b_s[l][:, 0][None, :], (chunk, groups))])
            mix, v_s = gating(z, a_g_v[l], ws_eff, b_eff, a_w, n_p // chunk)
            chunk_v_new.append(v_s.reshape(n_dec, 1, a_w))
            qm_col = 2 * a_w // mem_w
        else:
            b = l - n_a
            z = matmul([h], [b_w_in[b].astype(_BF)], tm=832, tn=512, out_dtype=_F32, name="b_in")
            cq = rmsnorm(z[:, :q_lora], b_g_cq[b], 416)
            w3 = b_w_qb[b].reshape(q_lora, heads, nope + rope)
            w_r = w3[:, :, nope:]
            w_rs = jnp.concatenate([w_r[:, :, rope // 2:], w_r[:, :, :rope // 2]], axis=2)
            w_q = jnp.concatenate([w3[:, :, :nope].reshape(q_lora, heads * nope),
                                   w_r.reshape(q_lora, heads * rope),
                                   w_rs.reshape(q_lora, heads * rope)], axis=1).astype(_BF)
            qraw = matmul([cq], [w_q], tm=832, tn=512, out_dtype=_F32, name="q_up")
            qn, qr = mla_q_finish(qraw, jnp.tile(cos2, (1, 2)), jnp.tile(sin2, (1, 2)), b_g_q[b],
                                  heads=heads, nope=nope, rope=rope, tb=416)
            scale = float(nope + rope) ** -0.5
            ks_t = (ks_all[:n_p] * scale).reshape(n_batch, seq, heads // 2, 2).transpose(0, 2, 3, 1)
            ks_t = ks_t.reshape(n_batch * (heads // 2), 2, seq)
            mix_p = mla_prompt_attention(qn, qr, kn_all, krope_all.astype(_BF), ks_t, v_all,
                                         n_batch=n_batch, seq=seq, heads=heads, nope=nope, rope=rope,
                                         tq=512, tk=512)
            q_lat = absorb_q(qn[n_p:], kv_g_k[:nope], w_uk2)
            o_lat = _paged_latent_attention_jnp(
                q_lat.transpose(1, 0, 2), qr[n_p:].reshape(n_dec, heads, rope),
                c_all[n_p:], krope_all[n_p:], ks_all[n_p:], cache_ckv, cache_krope, cache_kscale,
                page_table, scale)
            mix_s = project_latent_out(o_lat.transpose(1, 0, 2), w_uv2)
            mix = jnp.concatenate([mix_p, mix_s], axis=0)
            qm_col = q_lora // mem_w

        mo_p = memattn_prompt(z, qm_col, mk.reshape(n_batch, mem_tokens, mem_w), mv.reshape(n_batch, mem_tokens, mem_w),
                              g_mem_q[l], n_batch=n_batch, seq=seq, tq=512, heads=mem_heads)
        qm_s = z[n_p:, qm_col * mem_w:(qm_col + 1) * mem_w].reshape(n_dec, 1, mem_w)
        mo_s = memattn_sample(qm_s, cache_mem_k, cache_mem_v, l, g_mem_q[l]).reshape(n_dec, mem_w)
        mo = jnp.concatenate([mo_p, mo_s], axis=0)
        w_o = w_out[l].astype(_BF)
        mix_w = mix.shape[1]
        x = matmul([mix, mo], [w_o[:mix_w], w_o[mix_w:]], tm=832, tn=512, out_dtype=_F32, res=x, name="w_out")
        x = peer_ffn(x, g_ffn[l], peer_wq[l], peer_subkeys[l], peer_u[l], peer_v[l], n_prompt=n_p)

    c_all, krope_all, ks_all = kv_out
    return (x[:n_p].reshape(n_batch, seq, d), x[n_p:].reshape(n_dec, 1, d),
            c_all[:n_p].reshape(n_batch, seq, kv_lora), krope_all[:n_p].reshape(n_batch, seq, rope),
            ks_all[:n_p].reshape(n_batch, seq, heads),
            jnp.stack(mem_k_new), jnp.stack(mem_v_new),
            c_all[n_p:].reshape(n_dec, 1, kv_lora), krope_all[n_p:].reshape(n_dec, 1, rope),
            ks_all[n_p:].reshape(n_dec, 1, heads),
            jnp.stack(chunk_v_new))
```

```python
import functools
import math

import jax
import jax.numpy as jnp
from jax import lax
from jax.experimental import pallas as pl
from jax.experimental.pallas import tpu as pltpu

EPS = 1e-6
ROPE_THETA = 10000.0
_BF = jnp.bfloat16
_F32 = jnp.float32
_MIB = 1024 * 1024
_SQRT_HALF = 0.7071067811865476
_LOG2E = 1.4426950408889634
_NT = (((1,), (1,)), ((), ()))


def _params(semantics, vmem_mib):
    return pltpu.CompilerParams(dimension_semantics=semantics, vmem_limit_bytes=vmem_mib * _MIB)


def _tile(n, prefs):
    for p in prefs:
        if n % p == 0:
            return p
    raise ValueError(f"no tile in {prefs} divides {n}")


def _gelu(x):
    return 0.5 * x * (1.0 + lax.erf(x * _SQRT_HALF))


def _rms(x, g):
    return x * lax.rsqrt(jnp.mean(x * x, axis=-1, keepdims=True) + EPS) * g


def _rmsnorm_body(x_ref, g_ref, o_ref):
    o_ref[...] = _rms(x_ref[...], g_ref[...]).astype(o_ref.dtype)


def rmsnorm(x, g, tm, col_block=0):
    m = x.shape[0]
    d = g.shape[0]
    return pl.pallas_call(
        _rmsnorm_body,
        grid=(m // tm,),
        in_specs=[pl.BlockSpec((tm, d), lambda i: (i, col_block)), pl.BlockSpec((1, d), lambda i: (0, 0))],
        out_specs=pl.BlockSpec((tm, d), lambda i: (i, 0)),
        out_shape=jax.ShapeDtypeStruct((m, d), _BF),
        compiler_params=_params(("parallel",), 40),
        name="rmsnorm",
    )(x, g.reshape(1, d))


def _mm_body(*refs, n_lhs, has_res, n_extra, epilogue):
    o_ref = refs[-1]
    acc = None
    for a_ref, w_ref in zip(refs[:n_lhs], refs[n_lhs:2 * n_lhs]):
        d = jnp.dot(a_ref[...], w_ref[...], preferred_element_type=_F32)
        acc = d if acc is None else acc + d
    pos = 2 * n_lhs
    if has_res:
        acc = acc + refs[pos][...]
        pos += 1
    if epilogue is not None:
        acc = epilogue(acc, *[r[...] for r in refs[pos:pos + n_extra]])
    o_ref[...] = acc.astype(o_ref.dtype)


def matmul(lhs, rhs, *, tm, tn, out_dtype, res=None, extras=(), epilogue=None, name="matmul"):
    m = lhs[0].shape[0]
    n = rhs[0].shape[1]
    in_specs = [pl.BlockSpec((tm, a.shape[1]), lambda i, j: (i, 0)) for a in lhs]
    in_specs += [pl.BlockSpec((w.shape[0], tn), lambda i, j: (0, j)) for w in rhs]
    args = list(lhs) + list(rhs)
    if res is not None:
        in_specs.append(pl.BlockSpec((tm, tn), lambda i, j: (i, j)))
        args.append(res)
    for e in extras:
        if e.shape[1] == n:
            in_specs.append(pl.BlockSpec((1, tn), lambda i, j: (0, j)))
        else:
            in_specs.append(pl.BlockSpec(e.shape, lambda i, j: (0, 0)))
        args.append(e)
    body = functools.partial(_mm_body, n_lhs=len(lhs), has_res=res is not None,
                             n_extra=len(extras), epilogue=epilogue)
    return pl.pallas_call(
        body,
        grid=(m // tm, n // tn),
        in_specs=in_specs,
        out_specs=pl.BlockSpec((tm, tn), lambda i, j: (i, j)),
        out_shape=jax.ShapeDtypeStruct((m, n), out_dtype),
        compiler_params=_params(("parallel", "arbitrary"), 48),
        name=name,
    )(*args)


def _headnorm_epilogue(acc, g, *, width):
    parts = [_rms(acc[:, c:c + width], g) for c in range(0, acc.shape[1], width)]
    return jnp.concatenate(parts, axis=-1)


def _gating_body(zu_ref, zv_ref, gv_ref, ws_ref, b_ref, mix_ref, v_ref, *, groups):
    u = _gelu(zu_ref[...])
    v = _rms(_gelu(zv_ref[...]), gv_ref[...])

    @pl.when(pl.program_id(0) == pl.num_programs(0) - 1)
    def _():
        v_ref[...] = v

    gd = v.shape[1] // groups
    b = b_ref[0]
    for g in range(groups):
        sl = slice(g * gd, (g + 1) * gd)
        mixed = jnp.dot(ws_ref[0, g].astype(_BF), v[:, sl].astype(_BF), preferred_element_type=_F32)
        mix_ref[:, sl] = (u[:, sl] * (mixed + b[:, g:g + 1])).astype(mix_ref.dtype)


def gating(z, g_v, ws_eff, b_eff, a_w, n_prompt_chunks):
    t = z.shape[0]
    _, groups, chunk, _ = ws_eff.shape
    n_chunks = t // chunk
    return pl.pallas_call(
        functools.partial(_gating_body, groups=groups),
        grid=(n_chunks,),
        in_specs=[
            pl.BlockSpec((chunk, a_w), lambda i: (i, 0)),
            pl.BlockSpec((chunk, a_w), lambda i: (i, 1)),
            pl.BlockSpec((1, a_w), lambda i: (0, 0)),
            pl.BlockSpec((1, groups, chunk, chunk), lambda i: (i // n_prompt_chunks, 0, 0, 0)),
            pl.BlockSpec((1, chunk, groups), lambda i: (i // n_prompt_chunks, 0, 0)),
        ],
        out_specs=[
            pl.BlockSpec((chunk, a_w), lambda i: (i, 0)),
            pl.BlockSpec((chunk, a_w), lambda i: (0, 0)),
        ],
        out_shape=[jax.ShapeDtypeStruct((t, a_w), _BF), jax.ShapeDtypeStruct((chunk, a_w), _F32)],
        compiler_params=_params(("arbitrary",), 40),
        name="gating",
    )(z, z, g_v.reshape(1, a_w), ws_eff, b_eff)


def _memattn_prompt_body(q_ref, k_ref, v_ref, g_ref, o_ref, *, heads):
    hd = q_ref.shape[1] // heads
    scale = hd ** -0.5
    for h in range(heads):
        sl = slice(h * hd, (h + 1) * hd)
        q = _rms(q_ref[:, sl], g_ref[...]).astype(_BF)
        s = lax.dot_general(q, k_ref[0, :, sl].astype(_BF), _NT, preferred_element_type=_F32) * scale
        e = jnp.exp(s - jnp.max(s, axis=-1, keepdims=True))
        p = (e / jnp.sum(e, axis=-1, keepdims=True)).astype(_BF)
        o_ref[:, sl] = jnp.dot(p, v_ref[0, :, sl].astype(_BF), preferred_element_type=_F32).astype(o_ref.dtype)


def memattn_prompt(z, col_block, mem_k, mem_v, g_q, *, n_batch, seq, tq, heads):
    mem_w = mem_k.shape[2]
    mt = mem_k.shape[1]
    nq = seq // tq
    return pl.pallas_call(
        functools.partial(_memattn_prompt_body, heads=heads),
        grid=(n_batch, nq),
        in_specs=[
            pl.BlockSpec((tq, mem_w), lambda n, i: (n * nq + i, col_block)),
            pl.BlockSpec((1, mt, mem_w), lambda n, i: (n, 0, 0)),
            pl.BlockSpec((1, mt, mem_w), lambda n, i: (n, 0, 0)),
            pl.BlockSpec((1, mem_w // heads), lambda n, i: (0, 0)),
        ],
        out_specs=pl.BlockSpec((tq, mem_w), lambda n, i: (n * nq + i, 0)),
        out_shape=jax.ShapeDtypeStruct((n_batch * seq, mem_w), _BF),
        compiler_params=_params(("parallel", "arbitrary"), 40),
        name="memattn_prompt",
    )(z, mem_k, mem_v, g_q.reshape(1, -1))


def _memattn_sample_body(q_ref, k_ref, v_ref, g_ref, o_ref):
    q = _rms(q_ref[0], g_ref[...])
    s = jnp.sum(k_ref[0, 0] * q[None], axis=-1, keepdims=True) * (q.shape[-1] ** -0.5)
    e = jnp.exp(s - jnp.max(s, axis=0, keepdims=True))
    p = e / jnp.sum(e, axis=0, keepdims=True)
    o_ref[0] = jnp.sum(p * v_ref[0, 0], axis=0)


def memattn_sample(qm, cache_k, cache_v, layer, g_q):
    _, n, mt, heads, hd = cache_k.shape
    return pl.pallas_call(
        _memattn_sample_body,
        grid=(n,),
        in_specs=[
            pl.BlockSpec((1, heads, hd), lambda i: (i, 0, 0)),
            pl.BlockSpec((1, 1, mt, heads, hd), lambda i: (layer, i, 0, 0, 0)),
            pl.BlockSpec((1, 1, mt, heads, hd), lambda i: (layer, i, 0, 0, 0)),
            pl.BlockSpec((1, hd), lambda i: (0, 0)),
        ],
        out_specs=pl.BlockSpec((1, heads, hd), lambda i: (i, 0, 0)),
        out_shape=jax.ShapeDtypeStruct((n, heads, hd), _F32),
        compiler_params=_params(("parallel",), 40),
        name="memattn_sample",
    )(qm, cache_k, cache_v, g_q.reshape(1, hd))


_TOPK = 16


def _topk16(x, on_pick):
    rows = x.shape[0]
    iota = lax.broadcasted_iota(jnp.int32, x.shape, 0).astype(_F32)
    for k in range(_TOPK):
        m = jnp.max(x, axis=0, keepdims=True)
        pos = jnp.min(jnp.where(x == m, iota, float(rows)), axis=0, keepdims=True)
        hit = iota == pos
        on_pick(k, m, hit)
        x = jnp.where(hit, -jnp.inf, x)


def _route_head(hh, q_ref, sk_ref, h1_ref, e1_ref, r2_ref, e2_ref):
    tb = q_ref.shape[0]
    nk, half = sk_ref.shape[1:]
    row16 = lax.broadcasted_iota(jnp.int32, (_TOPK, tb), 0).astype(_F32)
    row8 = lax.broadcasted_iota(jnp.int32, (8, tb), 0).astype(_F32)
    scores, ranks, tops, top_arr = [], [], [], []
    for h in range(2):
        col = (2 * hh + h) * half
        s = lax.dot_general(sk_ref[2 * hh + h], q_ref[:, col:col + half], _NT, preferred_element_type=_F32)
        state = {"rank": jnp.full((nk, tb), float(nk), _F32), "vals": [], "arr": jnp.zeros((_TOPK, tb), _F32)}

        def pick(k, m, hit, state=state):
            state["rank"] = jnp.where(hit, float(k), state["rank"])
            state["vals"].append(m)
            state["arr"] = jnp.where(row16 == float(k), m, state["arr"])

        _topk16(s, pick)
        scores.append(s)
        ranks.append(state["rank"])
        tops.append(state["vals"])
        top_arr.append(state["arr"])
    v2 = top_arr[1]
    blocks = [tops[0][0] + v2]
    for a in range(1, 8):
        blocks.append(jnp.where(row8 < float(_TOPK // (a + 1)), tops[0][a] + v2[:8], -jnp.inf))
    blocks.append(top_arr[0][8:] + tops[1][0])
    cand = jnp.concatenate(blocks, axis=0)
    best0 = tops[0][0] + tops[1][0]
    st = {"sel": jnp.zeros(cand.shape, _F32), "z": jnp.zeros((1, tb), _F32)}

    def pick2(k, m, hit):
        st["sel"] = jnp.where(hit, 1.0, st["sel"])
        st["z"] = st["z"] + jnp.exp(m - best0)

    _topk16(cand, pick2)
    sel = st["sel"]
    count = [jnp.sum(sel[:16], axis=0, keepdims=True)]
    count += [jnp.sum(sel[8 + 8 * a:16 + 8 * a], axis=0, keepdims=True) for a in range(1, 8)]
    count += [sel[72 + i:73 + i] for i in range(8)]
    h1 = jnp.zeros((nk, tb), _F32)
    for a in range(_TOPK):
        h1 = h1 + jnp.where(ranks[0] == float(a), count[a], 0.0)
    h1_ref[hh] = h1
    r2_ref[hh] = ranks[1]
    e1_ref[hh] = jnp.exp(scores[0] - tops[0][0])
    e2_ref[hh] = jnp.exp(scores[1] - tops[1][0]) / st["z"]


_ROUTE_HEADS_PER_STEP = 4


def _route_body(q_ref, sk_ref, h1_ref, e1_ref, r2_ref, e2_ref):
    for hh in range(h1_ref.shape[0]):
        _route_head(hh, q_ref, sk_ref, h1_ref, e1_ref, r2_ref, e2_ref)


def peer_route(q, subkeys, tb):
    t = q.shape[0]
    n2, nk, half = subkeys.shape
    heads = n2 // 2
    hp = _ROUTE_HEADS_PER_STEP
    out = jax.ShapeDtypeStruct((heads, nk, t), _F32)
    ospec = pl.BlockSpec((hp, nk, tb), lambda i, p: (p, 0, i))
    return pl.pallas_call(
        _route_body,
        grid=(t // tb, heads // hp),
        in_specs=[
            pl.BlockSpec((tb, hp * 2 * half), lambda i, p: (i, p)),
            pl.BlockSpec((hp * 2, nk, half), lambda i, p: (p, 0, 0)),
        ],
        out_specs=[ospec, ospec, ospec, ospec],
        out_shape=[out, out, out, out],
        compiler_params=_params(("parallel", "arbitrary"), 40),
        name="peer_route",
    )(q, subkeys)


_EXPERT_CHUNK = 256


def _peer_body(hf_ref, u_ref, v_ref, h1_ref, e1_ref, r2_ref, e2_ref, x_ref, o_ref, *, heads, nk):
    j = pl.program_id(1)

    @pl.when(j == 0)
    def _():
        o_ref[...] = x_ref[...]

    te = u_ref.shape[0]
    ck = min(_EXPERT_CHUNK, te)
    hf = hf_ref[...]
    ats = [lax.dot_general(u_ref[c * ck:(c + 1) * ck, :], hf, _NT, preferred_element_type=_F32)
           for c in range(te // ck)]
    for c, at in enumerate(ats):
        pieces = []
        for s in range(ck // nk):
            i1 = j * (te // nk) + c * (ck // nk) + s
            g = None
            for p in range(heads):
                h1row = h1_ref[p, pl.ds(i1, 1), :]
                e1row = e1_ref[p, pl.ds(i1, 1), :]
                term = jnp.where(r2_ref[p] < h1row, e1row * e2_ref[p], 0.0)
                g = term if g is None else g + term
            pieces.append(_gelu(at[s * nk:(s + 1) * nk, :]) * g)
        b = jnp.concatenate(pieces, axis=0).T.astype(_BF)
        o_ref[...] += jnp.dot(b, v_ref[c * ck:(c + 1) * ck, :], preferred_element_type=_F32)


def peer_dense(x, hf, u, v, route, *, tm, te, row_block0, n_row_blocks):
    t, d = hf.shape
    n_exp = u.shape[0]
    heads, nk, _ = route[0].shape
    once = pl.Buffered(1)
    rspec = pl.BlockSpec((heads, nk, tm), lambda i, j: (0, 0, row_block0 + i), pipeline_mode=once)
    in_specs = [
        pl.BlockSpec((tm, d), lambda i, j: (row_block0 + i, 0)),
        pl.BlockSpec((te, d), lambda i, j: (j, 0)),
        pl.BlockSpec((te, d), lambda i, j: (j, 0)),
        rspec, rspec, rspec, rspec,
        pl.BlockSpec((tm, d), lambda i, j: (row_block0 + i, 0), pipeline_mode=once),
    ]
    args = [hf, u, v, *route, x]
    return pl.pallas_call(
        functools.partial(_peer_body, heads=heads, nk=nk),
        grid=(n_row_blocks, n_exp // te),
        in_specs=in_specs,
        out_specs=pl.BlockSpec((tm, d), lambda i, j: (row_block0 + i, 0), pipeline_mode=once),
        out_shape=jax.ShapeDtypeStruct((t, d), _F32),
        input_output_aliases={len(args) - 1: 0},
        compiler_params=_params(("parallel", "arbitrary"), 58),
        name="peer_dense",
    )(*args)


def peer_ffn(x, g_ffn, wq, subkeys, u, v, *, n_prompt, tiles):
    t, d = x.shape
    heads, _, nk, half = subkeys.shape
    hf = rmsnorm(x, g_ffn, tiles["norm"])
    q = matmul([hf], [wq.astype(_BF)], tm=tiles["mm"], tn=512, out_dtype=_BF, name="peer_q")
    route = peer_route(q, subkeys.reshape(heads * 2, nk, half).astype(_BF), 128)
    u_b = u.astype(_BF)
    v_b = v.astype(_BF)
    tm = tiles["peer"]
    ts = t - n_prompt
    te = tiles["experts"]
    y = peer_dense(x, hf, u_b, v_b, route, tm=tm, te=te, row_block0=0, n_row_blocks=n_prompt // tm)
    return peer_dense(y, hf, u_b, v_b, route, tm=ts, te=te, row_block0=n_prompt // ts, n_row_blocks=1)


def _kv_body(ckr_ref, cs_ref, gc_ref, gkn_ref, gkr_ref, wuk_ref, wuv_ref,
             c_ref, krope_ref, ks_ref, kn_ref, v_ref, *, heads, qk_head):
    kv_lora = c_ref.shape[1]
    c = _rms(ckr_ref[:, :kv_lora], gc_ref[...])
    c_ref[...] = c
    cb = c.astype(_BF)
    blk = ckr_ref[:, kv_lora:]
    rope = blk.shape[1] // 2
    lane = lax.broadcasted_iota(jnp.int32, blk.shape, 1)
    ssq_r = jnp.sum(jnp.where(lane < rope, blk * blk, 0.0), axis=-1, keepdims=True)
    t = blk * gkr_ref[...] * cs_ref[...]
    krope_ref[...] = t[:, :rope] + t[:, rope:]
    kn = jnp.dot(cb, wuk_ref[...], preferred_element_type=_F32)
    nope = kn.shape[1] // heads
    lane_h = lax.broadcasted_iota(jnp.int32, (kn.shape[0], 128), 1)
    ks = jnp.zeros((kn.shape[0], 128), _F32)
    for h in range(heads):
        x = kn[:, h * nope:(h + 1) * nope]
        ms = (jnp.sum(x * x, axis=-1, keepdims=True) + ssq_r) / qk_head
        ks = jnp.where(lane_h == h, lax.rsqrt(ms + EPS), ks)
        kn_ref[:, h * nope:(h + 1) * nope] = (x * gkn_ref[...]).astype(kn_ref.dtype)
    ks_ref[...] = ks[:, :heads]
    v_ref[...] = jnp.dot(cb, wuv_ref[...], preferred_element_type=_F32).astype(v_ref.dtype)


def latent_kv(ckr, cs_tab, g_c, g_k, w_uk, w_uv, *, heads, tb):
    t, w = ckr.shape
    kv_lora = w_uk.shape[0]
    rope = (w - kv_lora) // 2
    nope = w_uk.shape[1] // heads
    g_kr = g_k[nope:]
    gkr2 = jnp.concatenate([g_kr, jnp.concatenate([g_kr[rope // 2:], g_kr[:rope // 2]])]).reshape(1, 2 * rope)
    full = lambda i: (0, 0)
    row = lambda i: (i, 0)
    return pl.pallas_call(
        functools.partial(_kv_body, heads=heads, qk_head=float(nope + rope)),
        grid=(t // tb,),
        in_specs=[
            pl.BlockSpec((tb, w), row),
            pl.BlockSpec((tb, 2 * rope), row),
            pl.BlockSpec((1, kv_lora), full),
            pl.BlockSpec((1, nope), full),
            pl.BlockSpec((1, 2 * rope), full),
            pl.BlockSpec(w_uk.shape, full),
            pl.BlockSpec(w_uv.shape, full),
        ],
        out_specs=[
            pl.BlockSpec((tb, kv_lora), row),
            pl.BlockSpec((tb, rope), row),
            pl.BlockSpec((tb, heads), row),
            pl.BlockSpec((tb, w_uk.shape[1]), row),
            pl.BlockSpec((tb, w_uv.shape[1]), row),
        ],
        out_shape=[
            jax.ShapeDtypeStruct((t, kv_lora), _F32),
            jax.ShapeDtypeStruct((t, rope), _F32),
            jax.ShapeDtypeStruct((t, heads), _F32),
            jax.ShapeDtypeStruct((t, w_uk.shape[1]), _BF),
            jax.ShapeDtypeStruct((t, w_uv.shape[1]), _BF),
        ],
        compiler_params=_params(("parallel",), 48),
        name="latent_kv",
    )(ckr, cs_tab, g_c.reshape(1, -1), g_k[:nope].reshape(1, -1), gkr2, w_uk, w_uv)


def _q_body(q_ref, c2_ref, s2_ref, gn_ref, gr_ref, grs_ref, qn_ref, qr_ref, *, heads, nope, rope):
    n_all = heads * nope
    r_all = heads * rope
    qk_head = float(nope + rope)
    lane = lax.broadcasted_iota(jnp.int32, (q_ref.shape[0], 2 * rope), 1)
    first = lane < rope
    for j in range(heads // 2):
        n0 = q_ref[:, (2 * j) * nope:(2 * j + 1) * nope]
        n1 = q_ref[:, (2 * j + 1) * nope:(2 * j + 2) * nope]
        r = q_ref[:, n_all + j * 2 * rope:n_all + (j + 1) * 2 * rope]
        rs = q_ref[:, n_all + r_all + j * 2 * rope:n_all + r_all + (j + 1) * 2 * rope]
        r2 = r * r
        ss0 = jnp.sum(n0 * n0, axis=-1, keepdims=True) + jnp.sum(jnp.where(first, r2, 0.0), axis=-1, keepdims=True)
        ss1 = jnp.sum(n1 * n1, axis=-1, keepdims=True) + jnp.sum(jnp.where(first, 0.0, r2), axis=-1, keepdims=True)
        inv0 = lax.rsqrt(ss0 / qk_head + EPS)
        inv1 = lax.rsqrt(ss1 / qk_head + EPS)
        qn_ref[:, (2 * j) * nope:(2 * j + 1) * nope] = n0 * inv0 * gn_ref[...]
        qn_ref[:, (2 * j + 1) * nope:(2 * j + 2) * nope] = n1 * inv1 * gn_ref[...]
        inv_r = jnp.where(first, inv0, inv1)
        qr_ref[:, j * 2 * rope:(j + 1) * 2 * rope] = (
            r * inv_r * gr_ref[...] * c2_ref[...] + rs * inv_r * grs_ref[...] * s2_ref[...])


def mla_q_finish(qraw, c2_tab, s2_tab, g_q, *, heads, nope, rope, tb):
    t, w = qraw.shape
    g_r = g_q[nope:]
    g_rs = jnp.concatenate([g_r[rope // 2:], g_r[:rope // 2]])
    row = lambda i: (i, 0)
    full = lambda i: (0, 0)
    return pl.pallas_call(
        functools.partial(_q_body, heads=heads, nope=nope, rope=rope),
        grid=(t // tb,),
        in_specs=[
            pl.BlockSpec((tb, w), row),
            pl.BlockSpec((tb, 2 * rope), row),
            pl.BlockSpec((tb, 2 * rope), row),
            pl.BlockSpec((1, nope), full),
            pl.BlockSpec((1, 2 * rope), full),
            pl.BlockSpec((1, 2 * rope), full),
        ],
        out_specs=[pl.BlockSpec((tb, heads * nope), row), pl.BlockSpec((tb, heads * rope), row)],
        out_shape=[jax.ShapeDtypeStruct((t, heads * nope), _F32), jax.ShapeDtypeStruct((t, heads * rope), _F32)],
        compiler_params=_params(("parallel",), 48),
        name="mla_q_finish",
    )(qraw, c2_tab, s2_tab, g_q[:nope].reshape(1, -1),
      jnp.tile(g_r, 2).reshape(1, -1), jnp.tile(g_rs, 2).reshape(1, -1))


def _flash_body(qn_ref, qr_ref, kn_ref, kr_ref, ks_ref, v_ref, o_ref, *, nope, rope, tk):
    tq = qn_ref.shape[0]
    qi = pl.program_id(2)
    per_q = tq // tk
    row = lax.broadcasted_iota(jnp.int32, (tq, tk), 0)
    col = lax.broadcasted_iota(jnp.int32, (tq, tk), 1)
    for hh in range(2):
        qn = qn_ref[:, hh * nope:(hh + 1) * nope].astype(_BF)
        qr = qr_ref[:, hh * rope:(hh + 1) * rope].astype(_BF)

        def step(j, carry, mask_offset=None, hh=hh, qn=qn, qr=qr):
            m, l, acc = carry
            rows = pl.ds(pl.multiple_of(j * tk, tk), tk)
            s = lax.dot_general(qn, kn_ref[rows, hh * nope:(hh + 1) * nope], _NT, preferred_element_type=_F32)
            s = s + lax.dot_general(qr, kr_ref[rows, :], _NT, preferred_element_type=_F32)
            s = s * ks_ref[0, hh:hh + 1, rows]
            if mask_offset is not None:
                s = jnp.where(col + mask_offset <= row, s, -jnp.inf)
            m_new = jnp.maximum(m, jnp.max(s, axis=-1, keepdims=True))
            alpha = jnp.exp2(m - m_new)
            p = jnp.exp2(s - m_new)
            l = l * alpha + jnp.sum(p, axis=-1, keepdims=True)
            acc = acc * alpha + jnp.dot(p.astype(_BF), v_ref[rows, hh * nope:(hh + 1) * nope],
                                        preferred_element_type=_F32)
            return m_new, l, acc

        carry = (jnp.full((tq, 1), -jnp.inf, _F32), jnp.zeros((tq, 1), _F32), jnp.zeros((tq, nope), _F32))
        carry = lax.fori_loop(0, qi * per_q, step, carry)
        for d in range(per_q):
            carry = step(qi * per_q + d, carry, mask_offset=d * tk)
        m, l, acc = carry
        o_ref[:, hh * nope:(hh + 1) * nope] = (acc / l).astype(o_ref.dtype)


def mla_prompt_attention(qn, qr, kn, krope, ks_t, v, *, n_batch, seq, heads, nope, rope, tq, tk):
    nq = seq // tq
    hp = heads // 2
    return pl.pallas_call(
        functools.partial(_flash_body, nope=nope, rope=rope, tk=tk),
        grid=(n_batch, hp, nq),
        in_specs=[
            pl.BlockSpec((tq, 2 * nope), lambda n, h, i: (n * nq + i, h)),
            pl.BlockSpec((tq, 2 * rope), lambda n, h, i: (n * nq + i, h)),
            pl.BlockSpec((seq, 2 * nope), lambda n, h, i: (n, h)),
            pl.BlockSpec((seq, rope), lambda n, h, i: (n, 0)),
            pl.BlockSpec((1, 2, seq), lambda n, h, i: (n * hp + h, 0, 0)),
            pl.BlockSpec((seq, 2 * nope), lambda n, h, i: (n, h)),
        ],
        out_specs=pl.BlockSpec((tq, 2 * nope), lambda n, h, i: (n * nq + i, h)),
        out_shape=jax.ShapeDtypeStruct((n_batch * seq, heads * nope), _BF),
        compiler_params=_params(("parallel", "parallel", "arbitrary"), 40),
        name="mla_prompt_attention",
    )(qn, qr, kn, krope, ks_t, v)


def _head_nt_body(x_ref, g_ref, w_ref, o_ref):
    x = (x_ref[...] * g_ref[...]).astype(_BF)
    o_ref[...] = lax.dot_general(x, w_ref[...], _NT, preferred_element_type=_F32)


def absorb_q(qn_s, g_kn, w_uk):
    n = qn_s.shape[0]
    nope = g_kn.shape[0]
    kv_lora, w = w_uk.shape
    heads = w // nope
    return pl.pallas_call(
        _head_nt_body,
        grid=(heads,),
        in_specs=[
            pl.BlockSpec((n, nope), lambda h: (0, h)),
            pl.BlockSpec((1, nope), lambda h: (0, 0)),
            pl.BlockSpec((kv_lora, nope), lambda h: (0, h)),
        ],
        out_specs=pl.BlockSpec((n, kv_lora), lambda h: (0, h)),
        out_shape=jax.ShapeDtypeStruct((n, heads * kv_lora), _F32),
        compiler_params=_params(("parallel",), 40),
        name="absorb_q",
    )(qn_s, g_kn.reshape(1, nope), w_uk)


def _head_nn_body(x_ref, w_ref, o_ref):
    o_ref[...] = jnp.dot(x_ref[...].astype(_BF), w_ref[...], preferred_element_type=_F32).astype(o_ref.dtype)


def project_latent_out(o_lat, w_uv, heads):
    n = o_lat.shape[0]
    kv_lora = w_uv.shape[0]
    vh = w_uv.shape[1] // heads
    return pl.pallas_call(
        _head_nn_body,
        grid=(heads,),
        in_specs=[
            pl.BlockSpec((n, kv_lora), lambda h: (0, h)),
            pl.BlockSpec((kv_lora, vh), lambda h: (0, h)),
        ],
        out_specs=pl.BlockSpec((n, vh), lambda h: (0, h)),
        out_shape=jax.ShapeDtypeStruct((n, heads * vh), _BF),
        compiler_params=_params(("parallel",), 40),
        name="project_latent_out",
    )(o_lat, w_uv)


def _paged_body(pt_ref, qlat_ref, qr_ref, cnew_ref, krnew_ref, ksnew_ref, ckv_hbm, kr_hbm, ks_hbm,
                o_ref, c_buf, kr_buf, ks_buf, sem, *, pages_per_group, n_groups, scale):
    n = pl.program_id(0)
    n_samples = pl.num_programs(0)
    heads = qlat_ref.shape[1]
    page = c_buf.shape[2]
    keys = pages_per_group * page

    def group_copies(sample, g, slot):
        cps = []
        for p in range(pages_per_group):
            pg = pt_ref[sample, g * pages_per_group + p]
            cps.append(pltpu.make_async_copy(ckv_hbm.at[pg], c_buf.at[slot, p], sem.at[slot, 0]))
            cps.append(pltpu.make_async_copy(kr_hbm.at[pg], kr_buf.at[slot, p], sem.at[slot, 1]))
            cps.append(pltpu.make_async_copy(ks_hbm.at[pg], ks_buf.at[slot, p], sem.at[slot, 2]))
        return cps

    @pl.when(n == 0)
    def _():
        for cp in group_copies(0, 0, 0):
            cp.start()

    qlat32 = qlat_ref[0]
    qr32 = qr_ref[0]
    qlat = qlat32.astype(_BF)
    qr = qr32.astype(_BF)
    c_new = cnew_ref[0]
    s0 = (jnp.sum(qlat32 * c_new, axis=-1, keepdims=True)
          + jnp.sum(qr32 * krnew_ref[0], axis=-1, keepdims=True)) * ksnew_ref[0] * scale
    init = (s0, jnp.ones((heads, 1), _F32), jnp.broadcast_to(c_new, qlat32.shape))

    def group(g, carry):
        m, l, acc = carry
        slot = g % 2

        @pl.when(g + 1 < n_groups)
        def _():
            for cp in group_copies(n, g + 1, 1 - slot):
                cp.start()

        @pl.when(jnp.logical_and(g + 1 == n_groups, n + 1 < n_samples))
        def _():
            for cp in group_copies(n + 1, 0, 1 - slot):
                cp.start()

        for cp in group_copies(n, g, slot):
            cp.wait()
        c = c_buf[slot].reshape(keys, c_buf.shape[3]).astype(_BF)
        kr = kr_buf[slot].reshape(keys, kr_buf.shape[3]).astype(_BF)
        s = (lax.dot_general(qlat, c, _NT, preferred_element_type=_F32)
             + lax.dot_general(qr, kr, _NT, preferred_element_type=_F32))
        ks_t = jnp.concatenate([ks_buf[slot, p].T for p in range(pages_per_group)], axis=1)
        s = s * (ks_t * scale)
        m_new = jnp.maximum(m, jnp.max(s, axis=-1, keepdims=True))
        alpha = jnp.exp(m - m_new)
        p_exp = jnp.exp(s - m_new)
        l = l * alpha + jnp.sum(p_exp, axis=-1, keepdims=True)
        acc = acc * alpha + jnp.dot(p_exp.astype(_BF), c, preferred_element_type=_F32)
        return m_new, l, acc

    m, l, acc = lax.fori_loop(0, n_groups, group, init)
    o_ref[0] = acc / l


def paged_latent_attention(q_lat, q_r, c_new, kr_new, ks_new, cache_ckv, cache_krope, cache_kscale,
                           page_table, scale, pages_per_group):
    n, heads, kv_lora = q_lat.shape
    rope = q_r.shape[2]
    page = cache_ckv.shape[1]
    n_pages = page_table.shape[1]
    n_groups = n_pages // pages_per_group
    assert n_groups * pages_per_group == n_pages and n_groups % 2 == 0
    per = lambda i, pt: (i, 0, 0)
    grid_spec = pltpu.PrefetchScalarGridSpec(
        num_scalar_prefetch=1,
        grid=(n,),
        in_specs=[
            pl.BlockSpec((1, heads, kv_lora), per),
            pl.BlockSpec((1, heads, rope), per),
            pl.BlockSpec((1, 1, kv_lora), per),
            pl.BlockSpec((1, 1, rope), per),
            pl.BlockSpec((1, heads, 1), per),
            pl.BlockSpec(memory_space=pl.ANY),
            pl.BlockSpec(memory_space=pl.ANY),
            pl.BlockSpec(memory_space=pl.ANY),
        ],
        out_specs=pl.BlockSpec((1, heads, kv_lora), per),
        scratch_shapes=[
            pltpu.VMEM((2, pages_per_group, page, kv_lora), _F32),
            pltpu.VMEM((2, pages_per_group, page, rope), _F32),
            pltpu.VMEM((2, pages_per_group, page, heads), _F32),
            pltpu.SemaphoreType.DMA((2, 3)),
        ],
    )
    return pl.pallas_call(
        functools.partial(_paged_body, pages_per_group=pages_per_group, n_groups=n_groups, scale=scale),
        grid_spec=grid_spec,
        out_shape=jax.ShapeDtypeStruct((n, heads, kv_lora), _F32),
        compiler_params=_params(("arbitrary",), 40),
        name="paged_latent_attention",
    )(page_table, q_lat, q_r, c_new, kr_new, ks_new, cache_ckv, cache_krope, cache_kscale)


def _rope_tables(pos, rope):
    inv = 1.0 / (ROPE_THETA ** (jnp.arange(0, rope, 2, dtype=_F32) / rope))
    ang = pos.astype(_F32)[:, None] * inv[None, :]
    return jnp.cos(ang), jnp.sin(ang)


def kernel(x_prompt, x_sample, cache_ckv, cache_krope, cache_kscale, cache_mem_k, cache_mem_v, page_table, mem_prompt, g_mix, g_ffn, g_mem_in, w_mem_k, w_mem_v, g_mem_q, g_mem_k, w_out, peer_wq, peer_subkeys, peer_u, peer_v, a_w_in, a_g_v, a_w_s, a_b_s, b_w_in, b_g_cq, b_w_qb, b_g_q, kv_g_in, kv_w_dkv, kv_w_kr, kv_g_c, kv_g_k, kv_w_uk, kv_w_uv):
    n_batch, seq, d = x_prompt.shape
    n_dec = x_sample.shape[0]
    assert x_sample.shape[1] == 1
    depth = g_mix.shape[0]
    n_a = a_w_in.shape[0]
    mem_tokens, mem_heads, mem_hd = cache_mem_k.shape[2:]
    mem_w = mem_heads * mem_hd
    a_w = a_g_v.shape[1]
    groups, chunk = a_w_s.shape[1], a_w_s.shape[2]
    kv_lora, heads, nope = kv_w_uk.shape
    assert kv_w_uv.shape[2] == nope
    rope = kv_w_kr.shape[1]
    q_lora = b_g_cq.shape[1]
    n_p = n_batch * seq
    t = n_p + n_dec
    n_pages = page_table.shape[1]
    past = n_pages * cache_ckv.shape[1]
    tiles = {
        "norm": _tile(t, (416, 320, 128)),
        "mm": _tile(t, (832, 640, 128)),
        "peer": _tile(n_p, (512, 256, 128)),
        "experts": 512,
        "seq": _tile(seq, (512, 256, 128)),
        "pages": _tile(n_pages // 2, (16, 8, 4, 2, 1)),
    }

    x = jnp.concatenate([x_prompt.reshape(n_p, d), x_sample.reshape(n_dec, d)], axis=0)
    mem = mem_prompt.reshape(n_batch * mem_tokens, d)

    cos_p, sin_p = _rope_tables(jnp.arange(seq), rope)
    cos_s, sin_s = _rope_tables(past + jnp.arange(1), rope)
    cos = jnp.concatenate([jnp.tile(cos_p, (n_batch, 1)), jnp.tile(cos_s, (n_dec, 1))], axis=0)
    sin = jnp.concatenate([jnp.tile(sin_p, (n_batch, 1)), jnp.tile(sin_s, (n_dec, 1))], axis=0)
    cos2 = jnp.concatenate([cos, cos], axis=1)
    sin2 = jnp.concatenate([-sin, sin], axis=1)

    mem_k_new, mem_v_new, chunk_v_new = [], [], []
    kv_out = None
    for l in range(depth):
        hm = rmsnorm(mem, g_mem_in[l], _tile(mem.shape[0], (256, 128)))
        tmm = _tile(mem.shape[0], (512, 256, 128))
        mk = matmul([hm], [w_mem_k[l].astype(_BF)], tm=tmm, tn=512, out_dtype=_F32,
                    extras=[g_mem_k[l].reshape(1, mem_hd)],
                    epilogue=functools.partial(_headnorm_epilogue, width=mem_hd), name="mem_k")
        mv = matmul([hm], [w_mem_v[l].astype(_BF)], tm=tmm, tn=512, out_dtype=_F32, name="mem_v")
        mem_k_new.append(mk.reshape(n_batch, mem_tokens, mem_heads, mem_hd))
        mem_v_new.append(mv.reshape(n_batch, mem_tokens, mem_heads, mem_hd))

        if l == n_a:
            hk = rmsnorm(x, kv_g_in, tiles["norm"])
            kr_sw = jnp.concatenate([kv_w_kr[:, rope // 2:], kv_w_kr[:, :rope // 2]], axis=1)
            w_ckr = jnp.concatenate([kv_w_dkv, kv_w_kr, kr_sw], axis=1).astype(_BF)
            ckr = matmul([hk], [w_ckr], tm=tiles["mm"], tn=w_ckr.shape[1], out_dtype=_F32, name="kv_down")
            w_uk2 = kv_w_uk.reshape(kv_lora, heads * nope).astype(_BF)
            w_uv2 = kv_w_uv.reshape(kv_lora, -1).astype(_BF)
            c_all, krope_all, ks_all, kn_all, v_all = latent_kv(
                ckr, jnp.concatenate([cos2, sin2], axis=1), kv_g_c, kv_g_k, w_uk2, w_uv2,
                heads=heads, tb=tiles["norm"])
            kv_out = (c_all, krope_all, ks_all)

        h = rmsnorm(x, g_mix[l], tiles["norm"])
        if l < n_a:
            z = matmul([h], [a_w_in[l].astype(_BF)], tm=tiles["mm"], tn=512, out_dtype=_F32, name="a_in")
            tril = jnp.tril(a_w_s[l])
            diag = a_w_s[l][:, 0, 0][:, None, None] * jnp.eye(chunk, dtype=_F32)[None]
            ws_eff = jnp.stack([tril, diag])
            b_eff = jnp.stack([a_b_s[l].T, jnp.broadcast_to(a_b_s[l][:, 0][None, :], (chunk, groups))])
            mix, v_s = gating(z, a_g_v[l], ws_eff, b_eff, a_w, n_p // chunk)
            chunk_v_new.append(v_s.reshape(n_dec, 1, a_w))
            qm_col = 2 * a_w // mem_w
        else:
            b = l - n_a
            z = matmul([h], [b_w_in[b].astype(_BF)], tm=tiles["mm"], tn=512, out_dtype=_F32, name="b_in")
            cq = rmsnorm(z, b_g_cq[b], tiles["norm"])
            w3 = b_w_qb[b].reshape(q_lora, heads, nope + rope)
            w_r = w3[:, :, nope:]
            w_rs = jnp.concatenate([w_r[:, :, rope // 2:], w_r[:, :, :rope // 2]], axis=2)
            w_q = jnp.concatenate([w3[:, :, :nope].reshape(q_lora, heads * nope),
                                   w_r.reshape(q_lora, heads * rope),
                                   w_rs.reshape(q_lora, heads * rope)], axis=1).astype(_BF)
            qraw = matmul([cq], [w_q], tm=tiles["mm"], tn=512, out_dtype=_F32, name="q_up")
            qn, qr = mla_q_finish(qraw, jnp.tile(cos2, (1, 2)), jnp.tile(sin2, (1, 2)), b_g_q[b],
                                  heads=heads, nope=nope, rope=rope, tb=tiles["norm"])
            scale = float(nope + rope) ** -0.5
            ks_t = (ks_all[:n_p] * (scale * _LOG2E)).reshape(n_batch, seq, heads // 2, 2).transpose(0, 2, 3, 1)
            ks_t = ks_t.reshape(n_batch * (heads // 2), 2, seq)
            tq = tiles["seq"]
            mix_p = mla_prompt_attention(qn, qr, kn_all, krope_all.astype(_BF), ks_t, v_all,
                                         n_batch=n_batch, seq=seq, heads=heads, nope=nope, rope=rope,
                                         tq=tq, tk=min(tq, 256))
            q_lat = absorb_q(qn[n_p:], kv_g_k[:nope], w_uk2)
            o_lat = paged_latent_attention(
                q_lat.reshape(n_dec, heads, kv_lora), qr[n_p:].reshape(n_dec, heads, rope),
                c_all[n_p:].reshape(n_dec, 1, kv_lora), krope_all[n_p:].reshape(n_dec, 1, rope),
                ks_all[n_p:].reshape(n_dec, heads, 1), cache_ckv, cache_krope, cache_kscale,
                page_table, scale, tiles["pages"])
            mix_s = project_latent_out(o_lat.reshape(n_dec, heads * kv_lora), w_uv2, heads)
            mix = jnp.concatenate([mix_p, mix_s], axis=0)
            qm_col = q_lora // mem_w

        mo_p = memattn_prompt(z, qm_col, mk.reshape(n_batch, mem_tokens, mem_w), mv.reshape(n_batch, mem_tokens, mem_w),
                              g_mem_q[l], n_batch=n_batch, seq=seq, tq=tiles["seq"], heads=mem_heads)
        qm_s = z[n_p:, qm_col * mem_w:(qm_col + 1) * mem_w].reshape(n_dec, mem_heads, mem_hd)
        mo_s = memattn_sample(qm_s, cache_mem_k, cache_mem_v, l, g_mem_q[l]).reshape(n_dec, mem_w)
        mo = jnp.concatenate([mo_p, mo_s.astype(_BF)], axis=0)
        w_o = w_out[l].astype(_BF)
        mix_w = mix.shape[1]
        x = matmul([mix, mo], [w_o[:mix_w], w_o[mix_w:]], tm=tiles["mm"], tn=512, out_dtype=_F32, res=x, name="w_out")
        x = peer_ffn(x, g_ffn[l], peer_wq[l], peer_subkeys[l], peer_u[l], peer_v[l], n_prompt=n_p, tiles=tiles)

    c_all, krope_all, ks_all = kv_out
    return (x[:n_p].reshape(n_batch, seq, d), x[n_p:].reshape(n_dec, 1, d),
            c_all[:n_p].reshape(n_batch, seq, kv_lora), krope_all[:n_p].reshape(n_batch, seq, rope),
            ks_all[:n_p].reshape(n_batch, seq, heads),
            jnp.stack(mem_k_new), jnp.stack(mem_v_new),
            c_all[n_p:].reshape(n_dec, 1, kv_lora), krope_all[n_p:].reshape(n_dec, 1, rope),
            ks_all[n_p:].reshape(n_dec, 1, heads),
            jnp.stack(chunk_v_new))
```

```python
import functools
import math

import jax
import jax.numpy as jnp
from jax import lax
from jax.experimental import pallas as pl
from jax.experimental.pallas import tpu as pltpu

EPS = 1e-6
ROPE_THETA = 10000.0
_BF = jnp.bfloat16
_F32 = jnp.float32
_MIB = 1024 * 1024
_SQRT_HALF = 0.7071067811865476
_LOG2E = 1.4426950408889634
_NT = (((1,), (1,)), ((), ()))


def _params(semantics, vmem_mib):
    return pltpu.CompilerParams(dimension_semantics=semantics, vmem_limit_bytes=vmem_mib * _MIB)


def _tile(n, prefs):
    for p in prefs:
        if n % p == 0:
            return p
    raise ValueError(f"no tile in {prefs} divides {n}")


def _gelu(x):
    return 0.5 * x * (1.0 + lax.erf(x * _SQRT_HALF))


def _rms(x, g):
    return x * lax.rsqrt(jnp.mean(x * x, axis=-1, keepdims=True) + EPS) * g


def _rmsnorm_body(x_ref, g_ref, o_ref):
    o_ref[...] = _rms(x_ref[...], g_ref[...]).astype(o_ref.dtype)


def rmsnorm(x, g, tm, col_block=0):
    m = x.shape[0]
    d = g.shape[0]
    return pl.pallas_call(
        _rmsnorm_body,
        grid=(m // tm,),
        in_specs=[pl.BlockSpec((tm, d), lambda i: (i, col_block)), pl.BlockSpec((1, d), lambda i: (0, 0))],
        out_specs=pl.BlockSpec((tm, d), lambda i: (i, 0)),
        out_shape=jax.ShapeDtypeStruct((m, d), _BF),
        compiler_params=_params(("parallel",), 40),
        name="rmsnorm",
    )(x, g.reshape(1, d))


def _mm_body(*refs, n_lhs, has_res, n_extra, epilogue):
    o_ref = refs[-1]
    acc = None
    for a_ref, w_ref in zip(refs[:n_lhs], refs[n_lhs:2 * n_lhs]):
        d = jnp.dot(a_ref[...], w_ref[...], preferred_element_type=_F32)
        acc = d if acc is None else acc + d
    pos = 2 * n_lhs
    if has_res:
        acc = acc + refs[pos][...]
        pos += 1
    if epilogue is not None:
        acc = epilogue(acc, *[r[...] for r in refs[pos:pos + n_extra]])
    o_ref[...] = acc.astype(o_ref.dtype)


def matmul(lhs, rhs, *, tm, tn, out_dtype, res=None, extras=(), epilogue=None, name="matmul"):
    m = lhs[0].shape[0]
    n = rhs[0].shape[1]
    in_specs = [pl.BlockSpec((tm, a.shape[1]), lambda i, j: (i, 0)) for a in lhs]
    in_specs += [pl.BlockSpec((w.shape[0], tn), lambda i, j: (0, j)) for w in rhs]
    args = list(lhs) + list(rhs)
    if res is not None:
        in_specs.append(pl.BlockSpec((tm, tn), lambda i, j: (i, j)))
        args.append(res)
    for e in extras:
        if e.shape[1] == n:
            in_specs.append(pl.BlockSpec((1, tn), lambda i, j: (0, j)))
        else:
            in_specs.append(pl.BlockSpec(e.shape, lambda i, j: (0, 0)))
        args.append(e)
    body = functools.partial(_mm_body, n_lhs=len(lhs), has_res=res is not None,
                             n_extra=len(extras), epilogue=epilogue)
    return pl.pallas_call(
        body,
        grid=(m // tm, n // tn),
        in_specs=in_specs,
        out_specs=pl.BlockSpec((tm, tn), lambda i, j: (i, j)),
        out_shape=jax.ShapeDtypeStruct((m, n), out_dtype),
        compiler_params=_params(("parallel", "arbitrary"), 48),
        name=name,
    )(*args)


def _headnorm_epilogue(acc, g, *, width):
    parts = [_rms(acc[:, c:c + width], g) for c in range(0, acc.shape[1], width)]
    return jnp.concatenate(parts, axis=-1)


def _gating_body(zu_ref, zv_ref, gv_ref, ws_ref, b_ref, mix_ref, v_ref, *, groups):
    u = _gelu(zu_ref[...])
    v = _rms(_gelu(zv_ref[...]), gv_ref[...])

    @pl.when(pl.program_id(0) == pl.num_programs(0) - 1)
    def _():
        v_ref[...] = v

    gd = v.shape[1] // groups
    b = b_ref[0]
    for g in range(groups):
        sl = slice(g * gd, (g + 1) * gd)
        mixed = jnp.dot(ws_ref[0, g].astype(_BF), v[:, sl].astype(_BF), preferred_element_type=_F32)
        mix_ref[:, sl] = (u[:, sl] * (mixed + b[:, g:g + 1])).astype(mix_ref.dtype)


def gating(z, g_v, ws_eff, b_eff, a_w, n_prompt_chunks):
    t = z.shape[0]
    _, groups, chunk, _ = ws_eff.shape
    n_chunks = t // chunk
    return pl.pallas_call(
        functools.partial(_gating_body, groups=groups),
        grid=(n_chunks,),
        in_specs=[
            pl.BlockSpec((chunk, a_w), lambda i: (i, 0)),
            pl.BlockSpec((chunk, a_w), lambda i: (i, 1)),
            pl.BlockSpec((1, a_w), lambda i: (0, 0)),
            pl.BlockSpec((1, groups, chunk, chunk), lambda i: (i // n_prompt_chunks, 0, 0, 0)),
            pl.BlockSpec((1, chunk, groups), lambda i: (i // n_prompt_chunks, 0, 0)),
        ],
        out_specs=[
            pl.BlockSpec((chunk, a_w), lambda i: (i, 0)),
            pl.BlockSpec((chunk, a_w), lambda i: (0, 0)),
        ],
        out_shape=[jax.ShapeDtypeStruct((t, a_w), _BF), jax.ShapeDtypeStruct((chunk, a_w), _F32)],
        compiler_params=_params(("arbitrary",), 40),
        name="gating",
    )(z, z, g_v.reshape(1, a_w), ws_eff, b_eff)


def _memattn_prompt_body(q_ref, k_ref, v_ref, g_ref, o_ref, *, heads):
    hd = q_ref.shape[1] // heads
    scale = hd ** -0.5
    for h in range(heads):
        sl = slice(h * hd, (h + 1) * hd)
        q = _rms(q_ref[:, sl], g_ref[...]).astype(_BF)
        s = lax.dot_general(q, k_ref[0, :, sl].astype(_BF), _NT, preferred_element_type=_F32) * scale
        e = jnp.exp(s - jnp.max(s, axis=-1, keepdims=True))
        p = (e / jnp.sum(e, axis=-1, keepdims=True)).astype(_BF)
        o_ref[:, sl] = jnp.dot(p, v_ref[0, :, sl].astype(_BF), preferred_element_type=_F32).astype(o_ref.dtype)


def memattn_prompt(z, col_block, mem_k, mem_v, g_q, *, n_batch, seq, tq, heads):
    mem_w = mem_k.shape[2]
    mt = mem_k.shape[1]
    nq = seq // tq
    return pl.pallas_call(
        functools.partial(_memattn_prompt_body, heads=heads),
        grid=(n_batch, nq),
        in_specs=[
            pl.BlockSpec((tq, mem_w), lambda n, i: (n * nq + i, col_block)),
            pl.BlockSpec((1, mt, mem_w), lambda n, i: (n, 0, 0)),
            pl.BlockSpec((1, mt, mem_w), lambda n, i: (n, 0, 0)),
            pl.BlockSpec((1, mem_w // heads), lambda n, i: (0, 0)),
        ],
        out_specs=pl.BlockSpec((tq, mem_w), lambda n, i: (n * nq + i, 0)),
        out_shape=jax.ShapeDtypeStruct((n_batch * seq, mem_w), _BF),
        compiler_params=_params(("parallel", "arbitrary"), 40),
        name="memattn_prompt",
    )(z, mem_k, mem_v, g_q.reshape(1, -1))


def _memattn_sample_body(q_ref, k_ref, v_ref, g_ref, o_ref):
    q = _rms(q_ref[0], g_ref[...])
    s = jnp.sum(k_ref[0, 0] * q[None], axis=-1, keepdims=True) * (q.shape[-1] ** -0.5)
    e = jnp.exp(s - jnp.max(s, axis=0, keepdims=True))
    p = e / jnp.sum(e, axis=0, keepdims=True)
    o_ref[0] = jnp.sum(p * v_ref[0, 0], axis=0)


def memattn_sample(qm, cache_k, cache_v, layer, g_q):
    _, n, mt, heads, hd = cache_k.shape
    return pl.pallas_call(
        _memattn_sample_body,
        grid=(n,),
        in_specs=[
            pl.BlockSpec((1, heads, hd), lambda i: (i, 0, 0)),
            pl.BlockSpec((1, 1, mt, heads, hd), lambda i: (layer, i, 0, 0, 0)),
            pl.BlockSpec((1, 1, mt, heads, hd), lambda i: (layer, i, 0, 0, 0)),
            pl.BlockSpec((1, hd), lambda i: (0, 0)),
        ],
        out_specs=pl.BlockSpec((1, heads, hd), lambda i: (i, 0, 0)),
        out_shape=jax.ShapeDtypeStruct((n, heads, hd), _F32),
        compiler_params=_params(("parallel",), 40),
        name="memattn_sample",
    )(qm, cache_k, cache_v, g_q.reshape(1, hd))


_TOPK = 16


def _topk16(x, on_pick):
    rows = x.shape[0]
    iota = lax.broadcasted_iota(jnp.int32, x.shape, 0).astype(_F32)
    for k in range(_TOPK):
        m = jnp.max(x, axis=0, keepdims=True)
        pos = jnp.min(jnp.where(x == m, iota, float(rows)), axis=0, keepdims=True)
        hit = iota == pos
        on_pick(k, m, hit)
        x = jnp.where(hit, -jnp.inf, x)


def _route_head(hh, q_ref, sk_ref, h1_ref, e1_ref, r2_ref, e2_ref):
    tb = q_ref.shape[0]
    nk, half = sk_ref.shape[1:]
    row16 = lax.broadcasted_iota(jnp.int32, (_TOPK, tb), 0).astype(_F32)
    row8 = lax.broadcasted_iota(jnp.int32, (8, tb), 0).astype(_F32)
    scores, ranks, tops, top_arr = [], [], [], []
    for h in range(2):
        col = (2 * hh + h) * half
        s = lax.dot_general(sk_ref[2 * hh + h], q_ref[:, col:col + half], _NT, preferred_element_type=_F32)
        state = {"rank": jnp.full((nk, tb), float(nk), _F32), "vals": [], "arr": jnp.zeros((_TOPK, tb), _F32)}

        def pick(k, m, hit, state=state):
            state["rank"] = jnp.where(hit, float(k), state["rank"])
            state["vals"].append(m)
            state["arr"] = jnp.where(row16 == float(k), m, state["arr"])

        _topk16(s, pick)
        scores.append(s)
        ranks.append(state["rank"])
        tops.append(state["vals"])
        top_arr.append(state["arr"])
    v2 = top_arr[1]
    blocks = [tops[0][0] + v2]
    for a in range(1, 8):
        blocks.append(jnp.where(row8 < float(_TOPK // (a + 1)), tops[0][a] + v2[:8], -jnp.inf))
    blocks.append(top_arr[0][8:] + tops[1][0])
    cand = jnp.concatenate(blocks, axis=0)
    best0 = tops[0][0] + tops[1][0]
    st = {"sel": jnp.zeros(cand.shape, _F32), "z": jnp.zeros((1, tb), _F32)}

    def pick2(k, m, hit):
        st["sel"] = jnp.where(hit, 1.0, st["sel"])
        st["z"] = st["z"] + jnp.exp(m - best0)

    _topk16(cand, pick2)
    sel = st["sel"]
    count = [jnp.sum(sel[:16], axis=0, keepdims=True)]
    count += [jnp.sum(sel[8 + 8 * a:16 + 8 * a], axis=0, keepdims=True) for a in range(1, 8)]
    count += [sel[72 + i:73 + i] for i in range(8)]
    h1 = jnp.zeros((nk, tb), _F32)
    for a in range(_TOPK):
        h1 = h1 + jnp.where(ranks[0] == float(a), count[a], 0.0)
    h1_ref[hh] = h1
    r2_ref[hh] = ranks[1].astype(r2_ref.dtype)
    e1_ref[hh] = jnp.exp(scores[0] - tops[0][0])
    e2_ref[hh] = (jnp.exp(scores[1] - tops[1][0]) / st["z"]).astype(e2_ref.dtype)


_ROUTE_HEADS_PER_STEP = 4


def _route_body(q_ref, sk_ref, h1_ref, e1_ref, r2_ref, e2_ref):
    for hh in range(h1_ref.shape[0]):
        _route_head(hh, q_ref, sk_ref, h1_ref, e1_ref, r2_ref, e2_ref)


def peer_route(q, subkeys, tb):
    t = q.shape[0]
    n2, nk, half = subkeys.shape
    heads = n2 // 2
    hp = _ROUTE_HEADS_PER_STEP
    out = jax.ShapeDtypeStruct((heads, nk, t), _F32)
    out_b = jax.ShapeDtypeStruct((heads, nk, t), _BF)
    ospec = pl.BlockSpec((hp, nk, tb), lambda i, p: (p, 0, i))
    return pl.pallas_call(
        _route_body,
        grid=(t // tb, heads // hp),
        in_specs=[
            pl.BlockSpec((tb, hp * 2 * half), lambda i, p: (i, p)),
            pl.BlockSpec((hp * 2, nk, half), lambda i, p: (p, 0, 0)),
        ],
        out_specs=[ospec, ospec, ospec, ospec],
        out_shape=[out, out, out_b, out_b],
        compiler_params=_params(("parallel", "arbitrary"), 40),
        name="peer_route",
    )(q, subkeys)


_EXPERT_CHUNK = 256


def _peer_body(hf_ref, u_ref, v_ref, h1_ref, e1_ref, r2_ref, e2_ref, x_ref, o_ref, *, heads, nk):
    j = pl.program_id(1)

    @pl.when(j == 0)
    def _():
        o_ref[...] = x_ref[...]

    te = u_ref.shape[0]
    ck = min(_EXPERT_CHUNK, te)
    hf = hf_ref[...]
    ats = [lax.dot_general(u_ref[c * ck:(c + 1) * ck, :], hf, _NT, preferred_element_type=_F32)
           for c in range(te // ck)]
    tm = hf.shape[0]
    pack = 16
    for c, at in enumerate(ats):
        pieces = []
        for s in range(ck // nk):
            i1 = j * (te // nk) + c * (ck // nk) + s
            g = None
            for p in range(heads):
                h1row = jnp.broadcast_to(h1_ref[p, pl.ds(i1, 1), :], (pack, tm)).astype(_BF)
                e1row = jnp.broadcast_to(e1_ref[p, pl.ds(i1, 1), :], (pack, tm)).astype(_BF)
                r2 = r2_ref[p].reshape(nk // pack, pack, tm)
                e2 = e2_ref[p].reshape(nk // pack, pack, tm)
                term = jnp.where(r2 < h1row[None], e1row[None] * e2, jnp.zeros((), _BF))
                g = term if g is None else g + term
            act = _gelu(at[s * nk:(s + 1) * nk, :]).astype(_BF).reshape(nk // pack, pack, tm)
            pieces.append((act * g).reshape(nk, tm))
        b = jnp.concatenate(pieces, axis=0)
        o_ref[...] += lax.dot_general(b, v_ref[c * ck:(c + 1) * ck, :], (((0,), (0,)), ((), ())),
                                      preferred_element_type=_F32)


def peer_dense(x, hf, u, v, layer, route, *, tm, te, row_block0, n_row_blocks):
    t, d = hf.shape
    n_exp = u.shape[1]
    heads, nk, _ = route[0].shape
    once = pl.Buffered(1)
    rspec = pl.BlockSpec((heads, nk, tm), lambda i, j: (0, 0, row_block0 + i), pipeline_mode=once)
    in_specs = [
        pl.BlockSpec((tm, d), lambda i, j: (row_block0 + i, 0)),
        pl.BlockSpec((None, te, d), lambda i, j: (layer, j, 0)),
        pl.BlockSpec((None, te, d), lambda i, j: (layer, j, 0)),
        rspec, rspec, rspec, rspec,
        pl.BlockSpec((tm, d), lambda i, j: (row_block0 + i, 0), pipeline_mode=once),
    ]
    args = [hf, u, v, *route, x]
    return pl.pallas_call(
        functools.partial(_peer_body, heads=heads, nk=nk),
        grid=(n_row_blocks, n_exp // te),
        in_specs=in_specs,
        out_specs=pl.BlockSpec((tm, d), lambda i, j: (row_block0 + i, 0), pipeline_mode=once),
        out_shape=jax.ShapeDtypeStruct((t, d), _F32),
        input_output_aliases={len(args) - 1: 0},
        compiler_params=_params(("parallel", "arbitrary"), 58),
        name="peer_dense",
    )(*args)


def peer_ffn(x, g_ffn, wq, subkeys, u_b, v_b, layer, *, n_prompt, tiles):
    t, d = x.shape
    heads, _, nk, half = subkeys.shape
    hf = rmsnorm(x, g_ffn, tiles["norm"])
    q = matmul([hf], [wq.astype(_BF)], tm=tiles["mm"], tn=512, out_dtype=_BF, name="peer_q")
    route = peer_route(q, subkeys.reshape(heads * 2, nk, half).astype(_BF), 128)
    tm = tiles["peer"]
    ts = t - n_prompt
    te = tiles["experts"]
    y = peer_dense(x, hf, u_b, v_b, layer, route, tm=tm, te=te, row_block0=0, n_row_blocks=n_prompt // tm)
    return peer_dense(y, hf, u_b, v_b, layer, route, tm=ts, te=te, row_block0=n_prompt // ts, n_row_blocks=1)


def _kv_body(ckr_ref, cs_ref, gc_ref, gkn_ref, gkr_ref, wuk_ref, wuv_ref,
             c_ref, krope_ref, ks_ref, kn_ref, v_ref, *, heads, qk_head):
    kv_lora = c_ref.shape[1]
    c = _rms(ckr_ref[:, :kv_lora], gc_ref[...])
    c_ref[...] = c
    cb = c.astype(_BF)
    blk = ckr_ref[:, kv_lora:]
    rope = blk.shape[1] // 2
    lane = lax.broadcasted_iota(jnp.int32, blk.shape, 1)
    ssq_r = jnp.sum(jnp.where(lane < rope, blk * blk, 0.0), axis=-1, keepdims=True)
    t = blk * gkr_ref[...] * cs_ref[...]
    krope_ref[...] = t[:, :rope] + t[:, rope:]
    kn = jnp.dot(cb, wuk_ref[...], preferred_element_type=_F32)
    nope = kn.shape[1] // heads
    lane_h = lax.broadcasted_iota(jnp.int32, (kn.shape[0], 128), 1)
    ks = jnp.zeros((kn.shape[0], 128), _F32)
    for h in range(heads):
        x = kn[:, h * nope:(h + 1) * nope]
        ms = (jnp.sum(x * x, axis=-1, keepdims=True) + ssq_r) / qk_head
        ks = jnp.where(lane_h == h, lax.rsqrt(ms + EPS), ks)
        kn_ref[:, h * nope:(h + 1) * nope] = (x * gkn_ref[...]).astype(kn_ref.dtype)
    ks_ref[...] = ks[:, :heads]
    v_ref[...] = jnp.dot(cb, wuv_ref[...], preferred_element_type=_F32).astype(v_ref.dtype)


def latent_kv(ckr, cs_tab, g_c, g_k, w_uk, w_uv, *, heads, tb):
    t, w = ckr.shape
    kv_lora = w_uk.shape[0]
    rope = (w - kv_lora) // 2
    nope = w_uk.shape[1] // heads
    g_kr = g_k[nope:]
    gkr2 = jnp.concatenate([g_kr, jnp.concatenate([g_kr[rope // 2:], g_kr[:rope // 2]])]).reshape(1, 2 * rope)
    full = lambda i: (0, 0)
    row = lambda i: (i, 0)
    return pl.pallas_call(
        functools.partial(_kv_body, heads=heads, qk_head=float(nope + rope)),
        grid=(t // tb,),
        in_specs=[
            pl.BlockSpec((tb, w), row),
            pl.BlockSpec((tb, 2 * rope), row),
            pl.BlockSpec((1, kv_lora), full),
            pl.BlockSpec((1, nope), full),
            pl.BlockSpec((1, 2 * rope), full),
            pl.BlockSpec(w_uk.shape, full),
            pl.BlockSpec(w_uv.shape, full),
        ],
        out_specs=[
            pl.BlockSpec((tb, kv_lora), row),
            pl.BlockSpec((tb, rope), row),
            pl.BlockSpec((tb, heads), row),
            pl.BlockSpec((tb, w_uk.shape[1]), row),
            pl.BlockSpec((tb, w_uv.shape[1]), row),
        ],
        out_shape=[
            jax.ShapeDtypeStruct((t, kv_lora), _F32),
            jax.ShapeDtypeStruct((t, rope), _F32),
            jax.ShapeDtypeStruct((t, heads), _F32),
            jax.ShapeDtypeStruct((t, w_uk.shape[1]), _BF),
            jax.ShapeDtypeStruct((t, w_uv.shape[1]), _BF),
        ],
        compiler_params=_params(("parallel",), 48),
        name="latent_kv",
    )(ckr, cs_tab, g_c.reshape(1, -1), g_k[:nope].reshape(1, -1), gkr2, w_uk, w_uv)


def _q_body(q_ref, c2_ref, s2_ref, gn_ref, gr_ref, grs_ref, qn_ref, qr_ref, *, heads, nope, rope):
    n_all = heads * nope
    r_all = heads * rope
    qk_head = float(nope + rope)
    lane = lax.broadcasted_iota(jnp.int32, (q_ref.shape[0], 2 * rope), 1)
    first = lane < rope
    for j in range(heads // 2):
        n0 = q_ref[:, (2 * j) * nope:(2 * j + 1) * nope]
        n1 = q_ref[:, (2 * j + 1) * nope:(2 * j + 2) * nope]
        r = q_ref[:, n_all + j * 2 * rope:n_all + (j + 1) * 2 * rope]
        rs = q_ref[:, n_all + r_all + j * 2 * rope:n_all + r_all + (j + 1) * 2 * rope]
        r2 = r * r
        ss0 = jnp.sum(n0 * n0, axis=-1, keepdims=True) + jnp.sum(jnp.where(first, r2, 0.0), axis=-1, keepdims=True)
        ss1 = jnp.sum(n1 * n1, axis=-1, keepdims=True) + jnp.sum(jnp.where(first, 0.0, r2), axis=-1, keepdims=True)
        inv0 = lax.rsqrt(ss0 / qk_head + EPS)
        inv1 = lax.rsqrt(ss1 / qk_head + EPS)
        qn_ref[:, (2 * j) * nope:(2 * j + 1) * nope] = n0 * inv0 * gn_ref[...]
        qn_ref[:, (2 * j + 1) * nope:(2 * j + 2) * nope] = n1 * inv1 * gn_ref[...]
        inv_r = jnp.where(first, inv0, inv1)
        qr_ref[:, j * 2 * rope:(j + 1) * 2 * rope] = (
            r * inv_r * gr_ref[...] * c2_ref[...] + rs * inv_r * grs_ref[...] * s2_ref[...])


def mla_q_finish(qraw, c2_tab, s2_tab, g_q, *, heads, nope, rope, tb):
    t, w = qraw.shape
    g_r = g_q[nope:]
    g_rs = jnp.concatenate([g_r[rope // 2:], g_r[:rope // 2]])
    row = lambda i: (i, 0)
    full = lambda i: (0, 0)
    return pl.pallas_call(
        functools.partial(_q_body, heads=heads, nope=nope, rope=rope),
        grid=(t // tb,),
        in_specs=[
            pl.BlockSpec((tb, w), row),
            pl.BlockSpec((tb, 2 * rope), row),
            pl.BlockSpec((tb, 2 * rope), row),
            pl.BlockSpec((1, nope), full),
            pl.BlockSpec((1, 2 * rope), full),
            pl.BlockSpec((1, 2 * rope), full),
        ],
        out_specs=[pl.BlockSpec((tb, heads * nope), row), pl.BlockSpec((tb, heads * rope), row)],
        out_shape=[jax.ShapeDtypeStruct((t, heads * nope), _F32), jax.ShapeDtypeStruct((t, heads * rope), _F32)],
        compiler_params=_params(("parallel",), 48),
        name="mla_q_finish",
    )(qraw, c2_tab, s2_tab, g_q[:nope].reshape(1, -1),
      jnp.tile(g_r, 2).reshape(1, -1), jnp.tile(g_rs, 2).reshape(1, -1))


def _flash_body(qn_ref, qr_ref, kn_ref, kr_ref, ks_ref, v_ref, o_ref, *, nope, rope, n_q_tiles):
    tq = qn_ref.shape[0]
    qi = pl.program_id(2)
    causal = (lax.broadcasted_iota(jnp.int32, (tq, tq), 1) <= lax.broadcasted_iota(jnp.int32, (tq, tq), 0))

    def tile(lo):
        for hh in range(2):
            hs = slice(hh * nope, (hh + 1) * nope)
            q = jnp.concatenate([qn_ref[:, hs], qr_ref[:, hh * rope:(hh + 1) * rope]], axis=1).astype(_BF)

            def scores(k0, k1, hh=hh, hs=hs, q=q):
                k = jnp.concatenate([kn_ref[k0:k1, hs], kr_ref[k0:k1, :]], axis=1)
                s = lax.dot_general(q, k, _NT, preferred_element_type=_F32)
                return s * ks_ref[0, hh:hh + 1, k0:k1]

            s_d = jnp.where(causal, scores(lo, lo + tq), -jnp.inf)
            m = jnp.max(s_d, axis=-1, keepdims=True)
            if lo:
                s_f = scores(0, lo)
                m = jnp.maximum(m, jnp.max(s_f, axis=-1, keepdims=True))
            p = jnp.exp2(s_d - m)
            l = jnp.sum(p, axis=-1, keepdims=True)
            acc = jnp.dot(p.astype(_BF), v_ref[lo:lo + tq, hs], preferred_element_type=_F32)
            if lo:
                p = jnp.exp2(s_f - m)
                l = l + jnp.sum(p, axis=-1, keepdims=True)
                acc = acc + jnp.dot(p.astype(_BF), v_ref[0:lo, hs], preferred_element_type=_F32)
            o_ref[:, hs] = (acc / l).astype(o_ref.dtype)

    for t in range(n_q_tiles):
        pl.when(qi == t)(functools.partial(tile, t * tq))


def mla_prompt_attention(qn, qr, kn, krope, ks_t, v, *, n_batch, seq, heads, nope, rope, tq):
    nq = seq // tq
    hp = heads // 2
    return pl.pallas_call(
        functools.partial(_flash_body, nope=nope, rope=rope, n_q_tiles=nq),
        grid=(n_batch, hp, nq),
        in_specs=[
            pl.BlockSpec((tq, 2 * nope), lambda n, h, i: (n * nq + i, h)),
            pl.BlockSpec((tq, 2 * rope), lambda n, h, i: (n * nq + i, h)),
            pl.BlockSpec((seq, 2 * nope), lambda n, h, i: (n, h)),
            pl.BlockSpec((seq, rope), lambda n, h, i: (n, 0)),
            pl.BlockSpec((1, 2, seq), lambda n, h, i: (n * hp + h, 0, 0)),
            pl.BlockSpec((seq, 2 * nope), lambda n, h, i: (n, h)),
        ],
        out_specs=pl.BlockSpec((tq, 2 * nope), lambda n, h, i: (n * nq + i, h)),
        out_shape=jax.ShapeDtypeStruct((n_batch * seq, heads * nope), _BF),
        compiler_params=_params(("parallel", "parallel", "arbitrary"), 40),
        name="mla_prompt_attention",
    )(qn, qr, kn, krope, ks_t, v)


def _head_nt_body(x_ref, g_ref, w_ref, o_ref):
    x = (x_ref[...] * g_ref[...]).astype(_BF)
    o_ref[...] = lax.dot_general(x, w_ref[...], _NT, preferred_element_type=_F32)


def absorb_q(qn_s, g_kn, w_uk):
    n = qn_s.shape[0]
    nope = g_kn.shape[0]
    kv_lora, w = w_uk.shape
    heads = w // nope
    return pl.pallas_call(
        _head_nt_body,
        grid=(heads,),
        in_specs=[
            pl.BlockSpec((n, nope), lambda h: (0, h)),
            pl.BlockSpec((1, nope), lambda h: (0, 0)),
            pl.BlockSpec((kv_lora, nope), lambda h: (0, h)),
        ],
        out_specs=pl.BlockSpec((n, kv_lora), lambda h: (0, h)),
        out_shape=jax.ShapeDtypeStruct((n, heads * kv_lora), _F32),
        compiler_params=_params(("parallel",), 40),
        name="absorb_q",
    )(qn_s, g_kn.reshape(1, nope), w_uk)


def _head_nn_body(x_ref, w_ref, o_ref):
    o_ref[...] = jnp.dot(x_ref[...].astype(_BF), w_ref[...], preferred_element_type=_F32).astype(o_ref.dtype)


def project_latent_out(o_lat, w_uv, heads):
    n = o_lat.shape[0]
    kv_lora = w_uv.shape[0]
    vh = w_uv.shape[1] // heads
    return pl.pallas_call(
        _head_nn_body,
        grid=(heads,),
        in_specs=[
            pl.BlockSpec((n, kv_lora), lambda h: (0, h)),
            pl.BlockSpec((kv_lora, vh), lambda h: (0, h)),
        ],
        out_specs=pl.BlockSpec((n, vh), lambda h: (0, h)),
        out_shape=jax.ShapeDtypeStruct((n, heads * vh), _BF),
        compiler_params=_params(("parallel",), 40),
        name="project_latent_out",
    )(o_lat, w_uv)


def _paged_body(pt_ref, qlat_ref, qr_ref, cnew_ref, krnew_ref, ksnew_ref, ckv_hbm, kr_hbm, ks_hbm,
                o_ref, c_buf, kr_buf, ks_buf, sem, *, pages_per_group, n_groups, scale):
    n = pl.program_id(0)
    n_samples = pl.num_programs(0)
    heads = qlat_ref.shape[1]
    page = c_buf.shape[2]
    keys = pages_per_group * page

    def group_copies(sample, g, slot):
        cps = []
        for p in range(pages_per_group):
            pg = pt_ref[sample, g * pages_per_group + p]
            cps.append(pltpu.make_async_copy(ckv_hbm.at[pg], c_buf.at[slot, p], sem.at[slot, 0]))
            cps.append(pltpu.make_async_copy(kr_hbm.at[pg], kr_buf.at[slot, p], sem.at[slot, 1]))
            cps.append(pltpu.make_async_copy(ks_hbm.at[pg], ks_buf.at[slot, p], sem.at[slot, 2]))
        return cps

    @pl.when(n == 0)
    def _():
        for cp in group_copies(0, 0, 0):
            cp.start()

    qlat32 = qlat_ref[0]
    qr32 = qr_ref[0]
    qlat = qlat32.astype(_BF)
    qr = qr32.astype(_BF)
    c_new = cnew_ref[0]
    s0 = (jnp.sum(qlat32 * c_new, axis=-1, keepdims=True)
          + jnp.sum(qr32 * krnew_ref[0], axis=-1, keepdims=True)) * ksnew_ref[0] * scale
    init = (s0, jnp.ones((heads, 1), _F32), jnp.broadcast_to(c_new, qlat32.shape))

    def group(g, carry):
        m, l, acc = carry
        slot = g % 2

        @pl.when(g + 1 < n_groups)
        def _():
            for cp in group_copies(n, g + 1, 1 - slot):
                cp.start()

        @pl.when(jnp.logical_and(g + 1 == n_groups, n + 1 < n_samples))
        def _():
            for cp in group_copies(n + 1, 0, 1 - slot):
                cp.start()

        for cp in group_copies(n, g, slot):
            cp.wait()
        c = c_buf[slot].reshape(keys, c_buf.shape[3]).astype(_BF)
        pages = range(pages_per_group)
        s_r = jnp.concatenate([jnp.dot(qr, kr_buf[slot, p].astype(_BF), preferred_element_type=_F32)
                               for p in pages], axis=1)
        s = lax.dot_general(qlat, c, _NT, preferred_element_type=_F32) + s_r
        ks_t = jnp.concatenate([ks_buf[slot, p] for p in pages], axis=1)
        s = s * (ks_t * scale)
        m_new = jnp.maximum(m, jnp.max(s, axis=-1, keepdims=True))
        alpha = jnp.exp(m - m_new)
        p_exp = jnp.exp(s - m_new)
        l = l * alpha + jnp.sum(p_exp, axis=-1, keepdims=True)
        acc = acc * alpha + jnp.dot(p_exp.astype(_BF), c, preferred_element_type=_F32)
        return m_new, l, acc

    m, l, acc = lax.fori_loop(0, n_groups, group, init)
    o_ref[0] = acc / l


def paged_latent_attention(q_lat, q_r, c_new, kr_new, ks_new, cache_ckv, cache_krope_t, cache_kscale_t,
                           page_table, scale, pages_per_group):
    n, heads, kv_lora = q_lat.shape
    rope = q_r.shape[2]
    page = cache_ckv.shape[1]
    n_pages = page_table.shape[1]
    n_groups = n_pages // pages_per_group
    assert n_groups * pages_per_group == n_pages and n_groups % 2 == 0
    per = lambda i, pt: (i, 0, 0)
    grid_spec = pltpu.PrefetchScalarGridSpec(
        num_scalar_prefetch=1,
        grid=(n,),
        in_specs=[
            pl.BlockSpec((1, heads, kv_lora), per),
            pl.BlockSpec((1, heads, rope), per),
            pl.BlockSpec((1, 1, kv_lora), per),
            pl.BlockSpec((1, 1, rope), per),
            pl.BlockSpec((1, heads, 1), per),
            pl.BlockSpec(memory_space=pl.ANY),
            pl.BlockSpec(memory_space=pl.ANY),
            pl.BlockSpec(memory_space=pl.ANY),
        ],
        out_specs=pl.BlockSpec((1, heads, kv_lora), per),
        scratch_shapes=[
            pltpu.VMEM((2, pages_per_group, page, kv_lora), _F32),
            pltpu.VMEM((2, pages_per_group, rope, page), _F32),
            pltpu.VMEM((2, pages_per_group, heads, page), _F32),
            pltpu.SemaphoreType.DMA((2, 3)),
        ],
    )
    return pl.pallas_call(
        functools.partial(_paged_body, pages_per_group=pages_per_group, n_groups=n_groups, scale=scale),
        grid_spec=grid_spec,
        out_shape=jax.ShapeDtypeStruct((n, heads, kv_lora), _F32),
        compiler_params=_params(("arbitrary",), 40),
        name="paged_latent_attention",
    )(page_table, q_lat, q_r, c_new, kr_new, ks_new, cache_ckv, cache_krope_t, cache_kscale_t)


def _rope_tables(pos, rope):
    inv = 1.0 / (ROPE_THETA ** (jnp.arange(0, rope, 2, dtype=_F32) / rope))
    ang = pos.astype(_F32)[:, None] * inv[None, :]
    return jnp.cos(ang), jnp.sin(ang)


def kernel(x_prompt, x_sample, cache_ckv, cache_krope, cache_kscale, cache_mem_k, cache_mem_v, page_table, mem_prompt, g_mix, g_ffn, g_mem_in, w_mem_k, w_mem_v, g_mem_q, g_mem_k, w_out, peer_wq, peer_subkeys, peer_u, peer_v, a_w_in, a_g_v, a_w_s, a_b_s, b_w_in, b_g_cq, b_w_qb, b_g_q, kv_g_in, kv_w_dkv, kv_w_kr, kv_g_c, kv_g_k, kv_w_uk, kv_w_uv):
    n_batch, seq, d = x_prompt.shape
    n_dec = x_sample.shape[0]
    assert x_sample.shape[1] == 1
    depth = g_mix.shape[0]
    n_a = a_w_in.shape[0]
    mem_tokens, mem_heads, mem_hd = cache_mem_k.shape[2:]
    mem_w = mem_heads * mem_hd
    a_w = a_g_v.shape[1]
    groups, chunk = a_w_s.shape[1], a_w_s.shape[2]
    kv_lora, heads, nope = kv_w_uk.shape
    assert kv_w_uv.shape[2] == nope
    rope = kv_w_kr.shape[1]
    q_lora = b_g_cq.shape[1]
    n_p = n_batch * seq
    t = n_p + n_dec
    n_pages = page_table.shape[1]
    past = n_pages * cache_ckv.shape[1]
    tiles = {
        "norm": _tile(t, (416, 320, 128)),
        "mm": _tile(t, (832, 640, 128)),
        "peer": _tile(n_p, (512, 256, 128)),
        "experts": 512,
        "seq": _tile(seq, (512, 256, 128)),
        "pages": _tile(n_pages // 2, (16, 8, 4, 2, 1)),
    }

    x = jnp.concatenate([x_prompt.reshape(n_p, d), x_sample.reshape(n_dec, d)], axis=0)
    mem = mem_prompt.reshape(n_batch * mem_tokens, d)

    cos_p, sin_p = _rope_tables(jnp.arange(seq), rope)
    cos_s, sin_s = _rope_tables(past + jnp.arange(1), rope)
    cos = jnp.concatenate([jnp.tile(cos_p, (n_batch, 1)), jnp.tile(cos_s, (n_dec, 1))], axis=0)
    sin = jnp.concatenate([jnp.tile(sin_p, (n_batch, 1)), jnp.tile(sin_s, (n_dec, 1))], axis=0)
    cos2 = jnp.concatenate([cos, cos], axis=1)
    sin2 = jnp.concatenate([-sin, sin], axis=1)

    peer_u_b = peer_u.astype(_BF)
    peer_v_b = peer_v.astype(_BF)
    mem_k_new, mem_v_new, chunk_v_new = [], [], []
    kv_out = None
    for l in range(depth):
        hm = rmsnorm(mem, g_mem_in[l], _tile(mem.shape[0], (256, 128)))
        tmm = _tile(mem.shape[0], (512, 256, 128))
        mk = matmul([hm], [w_mem_k[l].astype(_BF)], tm=tmm, tn=512, out_dtype=_F32,
                    extras=[g_mem_k[l].reshape(1, mem_hd)],
                    epilogue=functools.partial(_headnorm_epilogue, width=mem_hd), name="mem_k")
        mv = matmul([hm], [w_mem_v[l].astype(_BF)], tm=tmm, tn=512, out_dtype=_F32, name="mem_v")
        mem_k_new.append(mk.reshape(n_batch, mem_tokens, mem_heads, mem_hd))
        mem_v_new.append(mv.reshape(n_batch, mem_tokens, mem_heads, mem_hd))

        if l == n_a:
            hk = rmsnorm(x, kv_g_in, tiles["norm"])
            kr_sw = jnp.concatenate([kv_w_kr[:, rope // 2:], kv_w_kr[:, :rope // 2]], axis=1)
            w_ckr = jnp.concatenate([kv_w_dkv, kv_w_kr, kr_sw], axis=1).astype(_BF)
            ckr = matmul([hk], [w_ckr], tm=tiles["mm"], tn=w_ckr.shape[1], out_dtype=_F32, name="kv_down")
            w_uk2 = kv_w_uk.reshape(kv_lora, heads * nope).astype(_BF)
            w_uv2 = kv_w_uv.reshape(kv_lora, -1).astype(_BF)
            c_all, krope_all, ks_all, kn_all, v_all = latent_kv(
                ckr, jnp.concatenate([cos2, sin2], axis=1), kv_g_c, kv_g_k, w_uk2, w_uv2,
                heads=heads, tb=tiles["norm"])
            kv_out = (c_all, krope_all, ks_all)

        h = rmsnorm(x, g_mix[l], tiles["norm"])
        if l < n_a:
            z = matmul([h], [a_w_in[l].astype(_BF)], tm=tiles["mm"], tn=512, out_dtype=_F32, name="a_in")
            tril = jnp.tril(a_w_s[l])
            diag = a_w_s[l][:, 0, 0][:, None, None] * jnp.eye(chunk, dtype=_F32)[None]
            ws_eff = jnp.stack([tril, diag])
            b_eff = jnp.stack([a_b_s[l].T, jnp.broadcast_to(a_b_s[l][:, 0][None, :], (chunk, groups))])
            mix, v_s = gating(z, a_g_v[l], ws_eff, b_eff, a_w, n_p // chunk)
            chunk_v_new.append(v_s.reshape(n_dec, 1, a_w))
            qm_col = 2 * a_w // mem_w
        else:
            b = l - n_a
            z = matmul([h], [b_w_in[b].astype(_BF)], tm=tiles["mm"], tn=512, out_dtype=_F32, name="b_in")
            cq = rmsnorm(z, b_g_cq[b], tiles["norm"])
            w3 = b_w_qb[b].reshape(q_lora, heads, nope + rope)
            w_r = w3[:, :, nope:]
            w_rs = jnp.concatenate([w_r[:, :, rope // 2:], w_r[:, :, :rope // 2]], axis=2)
            w_q = jnp.concatenate([w3[:, :, :nope].reshape(q_lora, heads * nope),
                                   w_r.reshape(q_lora, heads * rope),
                                   w_rs.reshape(q_lora, heads * rope)], axis=1).astype(_BF)
            qraw = matmul([cq], [w_q], tm=tiles["mm"], tn=512, out_dtype=_F32, name="q_up")
            qn, qr = mla_q_finish(qraw, jnp.tile(cos2, (1, 2)), jnp.tile(sin2, (1, 2)), b_g_q[b],
                                  heads=heads, nope=nope, rope=rope, tb=tiles["norm"])
            scale = float(nope + rope) ** -0.5
            ks_t = (ks_all[:n_p] * (scale * _LOG2E)).reshape(n_batch, seq, heads // 2, 2).transpose(0, 2, 3, 1)
            ks_t = ks_t.reshape(n_batch * (heads // 2), 2, seq)
            tq = tiles["seq"]
            mix_p = mla_prompt_attention(qn, qr, kn_all, krope_all.astype(_BF), ks_t, v_all,
                                         n_batch=n_batch, seq=seq, heads=heads, nope=nope, rope=rope,
                                         tq=tq)
            q_lat = absorb_q(qn[n_p:], kv_g_k[:nope], w_uk2)
            o_lat = paged_latent_attention(
                q_lat.reshape(n_dec, heads, kv_lora), qr[n_p:].reshape(n_dec, heads, rope),
                c_all[n_p:].reshape(n_dec, 1, kv_lora), krope_all[n_p:].reshape(n_dec, 1, rope),
                ks_all[n_p:].reshape(n_dec, heads, 1), cache_ckv,
                jnp.transpose(cache_krope, (0, 2, 1)), jnp.transpose(cache_kscale, (0, 2, 1)),
                page_table, scale, tiles["pages"])
            mix_s = project_latent_out(o_lat.reshape(n_dec, heads * kv_lora), w_uv2, heads)
            mix = jnp.concatenate([mix_p, mix_s], axis=0)
            qm_col = q_lora // mem_w

        mo_p = memattn_prompt(z, qm_col, mk.reshape(n_batch, mem_tokens, mem_w), mv.reshape(n_batch, mem_tokens, mem_w),
                              g_mem_q[l], n_batch=n_batch, seq=seq, tq=tiles["seq"], heads=mem_heads)
        qm_s = z[n_p:, qm_col * mem_w:(qm_col + 1) * mem_w].reshape(n_dec, mem_heads, mem_hd)
        mo_s = memattn_sample(qm_s, cache_mem_k, cache_mem_v, l, g_mem_q[l]).reshape(n_dec, mem_w)
        mo = jnp.concatenate([mo_p, mo_s.astype(_BF)], axis=0)
        w_o = w_out[l].astype(_BF)
        mix_w = mix.shape[1]
        x = matmul([mix, mo], [w_o[:mix_w], w_o[mix_w:]], tm=tiles["mm"], tn=512, out_dtype=_F32, res=x, name="w_out")
        x = peer_ffn(x, g_ffn[l], peer_wq[l], peer_subkeys[l], peer_u_b, peer_v_b, l, n_prompt=n_p, tiles=tiles)

    c_all, krope_all, ks_all = kv_out
    return (x[:n_p].reshape(n_batch, seq, d), x[n_p:].reshape(n_dec, 1, d),
            c_all[:n_p].reshape(n_batch, seq, kv_lora), krope_all[:n_p].reshape(n_batch, seq, rope),
            ks_all[:n_p].reshape(n_batch, seq, heads),
            jnp.stack(mem_k_new), jnp.stack(mem_v_new),
            c_all[n_p:].reshape(n_dec, 1, kv_lora), krope_all[n_p:].reshape(n_dec, 1, rope),
            ks_all[n_p:].reshape(n_dec, 1, heads),
            jnp.stack(chunk_v_new))
```

```python
import functools
import math

import jax
import jax.numpy as jnp
from jax import lax
from jax.experimental import pallas as pl
from jax.experimental.pallas import tpu as pltpu

EPS = 1e-6
ROPE_THETA = 10000.0
_BF = jnp.bfloat16
_F32 = jnp.float32
_MIB = 1024 * 1024
_SQRT_HALF = 0.7071067811865476
_LOG2E = 1.4426950408889634
_NT = (((1,), (1,)), ((), ()))


def _params(semantics, vmem_mib):
    return pltpu.CompilerParams(dimension_semantics=semantics, vmem_limit_bytes=vmem_mib * _MIB)


def _tile(n, prefs):
    for p in prefs:
        if n % p == 0:
            return p
    raise ValueError(f"no tile in {prefs} divides {n}")


def _gelu(x):
    return 0.5 * x * (1.0 + lax.erf(x * _SQRT_HALF))


def _rms(x, g):
    return x * lax.rsqrt(jnp.mean(x * x, axis=-1, keepdims=True) + EPS) * g


def _rmsnorm_body(x_ref, g_ref, o_ref):
    o_ref[...] = _rms(x_ref[...], g_ref[...]).astype(o_ref.dtype)


def rmsnorm(x, g, tm, col_block=0):
    m = x.shape[0]
    d = g.shape[0]
    return pl.pallas_call(
        _rmsnorm_body,
        grid=(m // tm,),
        in_specs=[pl.BlockSpec((tm, d), lambda i: (i, col_block)), pl.BlockSpec((1, d), lambda i: (0, 0))],
        out_specs=pl.BlockSpec((tm, d), lambda i: (i, 0)),
        out_shape=jax.ShapeDtypeStruct((m, d), _BF),
        compiler_params=_params(("parallel",), 40),
        name="rmsnorm",
    )(x, g.reshape(1, d))


def _mm_body(*refs, n_lhs, has_res, n_extra, epilogue):
    o_ref = refs[-1]
    acc = None
    for a_ref, w_ref in zip(refs[:n_lhs], refs[n_lhs:2 * n_lhs]):
        d = jnp.dot(a_ref[...], w_ref[...], preferred_element_type=_F32)
        acc = d if acc is None else acc + d
    pos = 2 * n_lhs
    if has_res:
        acc = acc + refs[pos][...]
        pos += 1
    if epilogue is not None:
        acc = epilogue(acc, *[r[...] for r in refs[pos:pos + n_extra]])
    o_ref[...] = acc.astype(o_ref.dtype)


def matmul(lhs, rhs, *, tm, tn, out_dtype, res=None, extras=(), epilogue=None, name="matmul"):
    m = lhs[0].shape[0]
    n = rhs[0].shape[1]
    in_specs = [pl.BlockSpec((tm, a.shape[1]), lambda i, j: (i, 0)) for a in lhs]
    in_specs += [pl.BlockSpec((w.shape[0], tn), lambda i, j: (0, j)) for w in rhs]
    args = list(lhs) + list(rhs)
    if res is not None:
        in_specs.append(pl.BlockSpec((tm, tn), lambda i, j: (i, j)))
        args.append(res)
    for e in extras:
        if e.shape[1] == n:
            in_specs.append(pl.BlockSpec((1, tn), lambda i, j: (0, j)))
        else:
            in_specs.append(pl.BlockSpec(e.shape, lambda i, j: (0, 0)))
        args.append(e)
    body = functools.partial(_mm_body, n_lhs=len(lhs), has_res=res is not None,
                             n_extra=len(extras), epilogue=epilogue)
    return pl.pallas_call(
        body,
        grid=(m // tm, n // tn),
        in_specs=in_specs,
        out_specs=pl.BlockSpec((tm, tn), lambda i, j: (i, j)),
        out_shape=jax.ShapeDtypeStruct((m, n), out_dtype),
        compiler_params=_params(("parallel", "arbitrary"), 48),
        name=name,
    )(*args)


def _headnorm_epilogue(acc, g, *, width):
    parts = [_rms(acc[:, c:c + width], g) for c in range(0, acc.shape[1], width)]
    return jnp.concatenate(parts, axis=-1)


def _gating_body(zu_ref, zv_ref, gv_ref, ws_ref, b_ref, mix_ref, v_ref, *, groups):
    u = _gelu(zu_ref[...])
    v = _rms(_gelu(zv_ref[...]), gv_ref[...])

    @pl.when(pl.program_id(0) == pl.num_programs(0) - 1)
    def _():
        v_ref[...] = v

    gd = v.shape[1] // groups
    b = b_ref[0]
    for g in range(groups):
        sl = slice(g * gd, (g + 1) * gd)
        mixed = jnp.dot(ws_ref[0, g].astype(_BF), v[:, sl].astype(_BF), preferred_element_type=_F32)
        mix_ref[:, sl] = (u[:, sl] * (mixed + b[:, g:g + 1])).astype(mix_ref.dtype)


def gating(z, g_v, ws_eff, b_eff, a_w, n_prompt_chunks):
    t = z.shape[0]
    _, groups, chunk, _ = ws_eff.shape
    n_chunks = t // chunk
    return pl.pallas_call(
        functools.partial(_gating_body, groups=groups),
        grid=(n_chunks,),
        in_specs=[
            pl.BlockSpec((chunk, a_w), lambda i: (i, 0)),
            pl.BlockSpec((chunk, a_w), lambda i: (i, 1)),
            pl.BlockSpec((1, a_w), lambda i: (0, 0)),
            pl.BlockSpec((1, groups, chunk, chunk), lambda i: (i // n_prompt_chunks, 0, 0, 0)),
            pl.BlockSpec((1, chunk, groups), lambda i: (i // n_prompt_chunks, 0, 0)),
        ],
        out_specs=[
            pl.BlockSpec((chunk, a_w), lambda i: (i, 0)),
            pl.BlockSpec((chunk, a_w), lambda i: (0, 0)),
        ],
        out_shape=[jax.ShapeDtypeStruct((t, a_w), _BF), jax.ShapeDtypeStruct((chunk, a_w), _F32)],
        compiler_params=_params(("arbitrary",), 40),
        name="gating",
    )(z, z, g_v.reshape(1, a_w), ws_eff, b_eff)


def _memattn_prompt_body(q_ref, k_ref, v_ref, g_ref, o_ref, *, heads):
    hd = q_ref.shape[1] // heads
    scale = hd ** -0.5
    for h in range(heads):
        sl = slice(h * hd, (h + 1) * hd)
        q = _rms(q_ref[:, sl], g_ref[...]).astype(_BF)
        s = lax.dot_general(q, k_ref[0, :, sl].astype(_BF), _NT, preferred_element_type=_F32) * scale
        e = jnp.exp(s - jnp.max(s, axis=-1, keepdims=True))
        p = (e / jnp.sum(e, axis=-1, keepdims=True)).astype(_BF)
        o_ref[:, sl] = jnp.dot(p, v_ref[0, :, sl].astype(_BF), preferred_element_type=_F32).astype(o_ref.dtype)


def memattn_prompt(z, col_block, mem_k, mem_v, g_q, *, n_batch, seq, tq, heads):
    mem_w = mem_k.shape[2]
    mt = mem_k.shape[1]
    nq = seq // tq
    return pl.pallas_call(
        functools.partial(_memattn_prompt_body, heads=heads),
        grid=(n_batch, nq),
        in_specs=[
            pl.BlockSpec((tq, mem_w), lambda n, i: (n * nq + i, col_block)),
            pl.BlockSpec((1, mt, mem_w), lambda n, i: (n, 0, 0)),
            pl.BlockSpec((1, mt, mem_w), lambda n, i: (n, 0, 0)),
            pl.BlockSpec((1, mem_w // heads), lambda n, i: (0, 0)),
        ],
        out_specs=pl.BlockSpec((tq, mem_w), lambda n, i: (n * nq + i, 0)),
        out_shape=jax.ShapeDtypeStruct((n_batch * seq, mem_w), _BF),
        compiler_params=_params(("parallel", "arbitrary"), 40),
        name="memattn_prompt",
    )(z, mem_k, mem_v, g_q.reshape(1, -1))


def _memattn_sample_body(q_ref, k_ref, v_ref, g_ref, o_ref):
    q = _rms(q_ref[0], g_ref[...])
    s = jnp.sum(k_ref[0, 0] * q[None], axis=-1, keepdims=True) * (q.shape[-1] ** -0.5)
    e = jnp.exp(s - jnp.max(s, axis=0, keepdims=True))
    p = e / jnp.sum(e, axis=0, keepdims=True)
    o_ref[0] = jnp.sum(p * v_ref[0, 0], axis=0)


def memattn_sample(qm, cache_k, cache_v, layer, g_q):
    _, n, mt, heads, hd = cache_k.shape
    return pl.pallas_call(
        _memattn_sample_body,
        grid=(n,),
        in_specs=[
            pl.BlockSpec((1, heads, hd), lambda i: (i, 0, 0)),
            pl.BlockSpec((1, 1, mt, heads, hd), lambda i: (layer, i, 0, 0, 0)),
            pl.BlockSpec((1, 1, mt, heads, hd), lambda i: (layer, i, 0, 0, 0)),
            pl.BlockSpec((1, hd), lambda i: (0, 0)),
        ],
        out_specs=pl.BlockSpec((1, heads, hd), lambda i: (i, 0, 0)),
        out_shape=jax.ShapeDtypeStruct((n, heads, hd), _F32),
        compiler_params=_params(("parallel",), 40),
        name="memattn_sample",
    )(qm, cache_k, cache_v, g_q.reshape(1, hd))


_TOPK = 16


def _topk16(x, on_pick):
    rows = x.shape[0]
    iota = lax.broadcasted_iota(jnp.int32, x.shape, 0).astype(_F32)
    for k in range(_TOPK):
        m = jnp.max(x, axis=0, keepdims=True)
        pos = jnp.min(jnp.where(x == m, iota, float(rows)), axis=0, keepdims=True)
        hit = iota == pos
        on_pick(k, m, hit)
        x = jnp.where(hit, -jnp.inf, x)


def _route_head(hh, q_ref, sk_ref, h1_ref, e1_ref, r2_ref, e2_ref):
    tb = q_ref.shape[0]
    nk, half = sk_ref.shape[1:]
    row16 = lax.broadcasted_iota(jnp.int32, (_TOPK, tb), 0).astype(_F32)
    row8 = lax.broadcasted_iota(jnp.int32, (8, tb), 0).astype(_F32)
    scores, ranks, tops, top_arr = [], [], [], []
    for h in range(2):
        col = (2 * hh + h) * half
        s = lax.dot_general(sk_ref[2 * hh + h], q_ref[:, col:col + half], _NT, preferred_element_type=_F32)
        state = {"rank": jnp.full((nk, tb), float(nk), _F32), "vals": [], "arr": jnp.zeros((_TOPK, tb), _F32)}

        def pick(k, m, hit, state=state):
            state["rank"] = jnp.where(hit, float(k), state["rank"])
            state["vals"].append(m)
            state["arr"] = jnp.where(row16 == float(k), m, state["arr"])

        _topk16(s, pick)
        scores.append(s)
        ranks.append(state["rank"])
        tops.append(state["vals"])
        top_arr.append(state["arr"])
    v2 = top_arr[1]
    blocks = [tops[0][0] + v2]
    for a in range(1, 8):
        blocks.append(jnp.where(row8 < float(_TOPK // (a + 1)), tops[0][a] + v2[:8], -jnp.inf))
    blocks.append(top_arr[0][8:] + tops[1][0])
    cand = jnp.concatenate(blocks, axis=0)
    best0 = tops[0][0] + tops[1][0]
    st = {"sel": jnp.zeros(cand.shape, _F32), "z": jnp.zeros((1, tb), _F32)}

    def pick2(k, m, hit):
        st["sel"] = jnp.where(hit, 1.0, st["sel"])
        st["z"] = st["z"] + jnp.exp(m - best0)

    _topk16(cand, pick2)
    sel = st["sel"]
    count = [jnp.sum(sel[:16], axis=0, keepdims=True)]
    count += [jnp.sum(sel[8 + 8 * a:16 + 8 * a], axis=0, keepdims=True) for a in range(1, 8)]
    count += [sel[72 + i:73 + i] for i in range(8)]
    h1 = jnp.zeros((nk, tb), _F32)
    for a in range(_TOPK):
        h1 = h1 + jnp.where(ranks[0] == float(a), count[a], 0.0)
    h1_ref[hh] = h1
    r2_ref[hh] = ranks[1].astype(r2_ref.dtype)
    e1_ref[hh] = jnp.exp(scores[0] - tops[0][0])
    e2_ref[hh] = (jnp.exp(scores[1] - tops[1][0]) / st["z"]).astype(e2_ref.dtype)


_ROUTE_HEADS_PER_STEP = 8


def _route_body(q_ref, sk_ref, h1_ref, e1_ref, r2_ref, e2_ref):
    for hh in range(h1_ref.shape[0]):
        _route_head(hh, q_ref, sk_ref, h1_ref, e1_ref, r2_ref, e2_ref)


def peer_route(q, subkeys, tb):
    t = q.shape[0]
    n2, nk, half = subkeys.shape
    heads = n2 // 2
    hp = _ROUTE_HEADS_PER_STEP
    out = jax.ShapeDtypeStruct((heads, nk, t), _F32)
    out_b = jax.ShapeDtypeStruct((heads, nk, t), _BF)
    ospec = pl.BlockSpec((hp, nk, tb), lambda i, p: (p, 0, i))
    return pl.pallas_call(
        _route_body,
        grid=(t // tb, heads // hp),
        in_specs=[
            pl.BlockSpec((tb, hp * 2 * half), lambda i, p: (i, p)),
            pl.BlockSpec((hp * 2, nk, half), lambda i, p: (p, 0, 0)),
        ],
        out_specs=[ospec, ospec, ospec, ospec],
        out_shape=[out, out, out_b, out_b],
        compiler_params=_params(("parallel", "arbitrary"), 40),
        name="peer_route",
    )(q, subkeys)


_EXPERT_CHUNK = 256


def _peer_body(hf_ref, u_ref, v_ref, h1_ref, e1_ref, r2_ref, e2_ref, x_ref, o_ref, *, heads, nk):
    j = pl.program_id(1)

    @pl.when(j == 0)
    def _():
        o_ref[...] = x_ref[...]

    te = u_ref.shape[0]
    ck = min(_EXPERT_CHUNK, te)
    hf = hf_ref[...]
    ats = [lax.dot_general(u_ref[c * ck:(c + 1) * ck, :], hf, _NT, preferred_element_type=_F32)
           for c in range(te // ck)]
    tm = hf.shape[0]
    pack = 16
    for c, at in enumerate(ats):
        pieces = []
        for s in range(ck // nk):
            i1 = j * (te // nk) + c * (ck // nk) + s
            g = None
            for p in range(heads):
                h1row = jnp.broadcast_to(h1_ref[p, pl.ds(i1, 1), :], (pack, tm)).astype(_BF)
                e1row = jnp.broadcast_to(e1_ref[p, pl.ds(i1, 1), :], (pack, tm)).astype(_BF)
                r2 = r2_ref[p].reshape(nk // pack, pack, tm)
                e2 = e2_ref[p].reshape(nk // pack, pack, tm)
                term = jnp.where(r2 < h1row[None], e1row[None] * e2, jnp.zeros((), _BF))
                g = term if g is None else g + term
            act = _gelu(at[s * nk:(s + 1) * nk, :]).astype(_BF).reshape(nk // pack, pack, tm)
            pieces.append((act * g).reshape(nk, tm))
        b = jnp.concatenate(pieces, axis=0)
        o_ref[...] += lax.dot_general(b, v_ref[c * ck:(c + 1) * ck, :], (((0,), (0,)), ((), ())),
                                      preferred_element_type=_F32)


def peer_dense(x, hf, u, v, layer, route, *, tm, te, row_block0, n_row_blocks):
    t, d = hf.shape
    n_exp = u.shape[1]
    heads, nk, _ = route[0].shape
    once = pl.Buffered(1)
    rspec = pl.BlockSpec((heads, nk, tm), lambda i, j: (0, 0, row_block0 + i), pipeline_mode=once)
    in_specs = [
        pl.BlockSpec((tm, d), lambda i, j: (row_block0 + i, 0)),
        pl.BlockSpec((None, te, d), lambda i, j: (layer, j, 0)),
        pl.BlockSpec((None, te, d), lambda i, j: (layer, j, 0)),
        rspec, rspec, rspec, rspec,
        pl.BlockSpec((tm, d), lambda i, j: (row_block0 + i, 0), pipeline_mode=once),
    ]
    args = [hf, u, v, *route, x]
    return pl.pallas_call(
        functools.partial(_peer_body, heads=heads, nk=nk),
        grid=(n_row_blocks, n_exp // te),
        in_specs=in_specs,
        out_specs=pl.BlockSpec((tm, d), lambda i, j: (row_block0 + i, 0), pipeline_mode=once),
        out_shape=jax.ShapeDtypeStruct((t, d), _F32),
        input_output_aliases={len(args) - 1: 0},
        compiler_params=_params(("parallel", "arbitrary"), 58),
        name="peer_dense",
    )(*args)


def peer_ffn(x, g_ffn, wq, subkeys, u_b, v_b, layer, *, n_prompt, tiles):
    t, d = x.shape
    heads, _, nk, half = subkeys.shape
    hf = rmsnorm(x, g_ffn, tiles["norm"])
    q = matmul([hf], [wq.astype(_BF)], tm=tiles["mm"], tn=512, out_dtype=_BF, name="peer_q")
    route = peer_route(q, subkeys.reshape(heads * 2, nk, half).astype(_BF), 128)
    tm = tiles["peer"]
    ts = t - n_prompt
    te = tiles["experts"]
    y = peer_dense(x, hf, u_b, v_b, layer, route, tm=tm, te=te, row_block0=0, n_row_blocks=n_prompt // tm)
    return peer_dense(y, hf, u_b, v_b, layer, route, tm=ts, te=te, row_block0=n_prompt // ts, n_row_blocks=1)


def _kv_body(ckr_ref, cs_ref, gc_ref, gkn_ref, gkr_ref, wuk_ref, wuv_ref,
             c_ref, krope_ref, ks_ref, kn_ref, v_ref, *, heads, qk_head):
    kv_lora = c_ref.shape[1]
    c = _rms(ckr_ref[:, :kv_lora], gc_ref[...])
    c_ref[...] = c
    cb = c.astype(_BF)
    blk = ckr_ref[:, kv_lora:]
    rope = blk.shape[1] // 2
    lane = lax.broadcasted_iota(jnp.int32, blk.shape, 1)
    ssq_r = jnp.sum(jnp.where(lane < rope, blk * blk, 0.0), axis=-1, keepdims=True)
    t = blk * gkr_ref[...] * cs_ref[...]
    krope_ref[...] = t[:, :rope] + t[:, rope:]
    kn = jnp.dot(cb, wuk_ref[...], preferred_element_type=_F32)
    nope = kn.shape[1] // heads
    lane_h = lax.broadcasted_iota(jnp.int32, (kn.shape[0], 128), 1)
    ks = jnp.zeros((kn.shape[0], 128), _F32)
    for h in range(heads):
        x = kn[:, h * nope:(h + 1) * nope]
        ms = (jnp.sum(x * x, axis=-1, keepdims=True) + ssq_r) / qk_head
        ks = jnp.where(lane_h == h, lax.rsqrt(ms + EPS), ks)
        kn_ref[:, h * nope:(h + 1) * nope] = (x * gkn_ref[...]).astype(kn_ref.dtype)
    ks_ref[...] = ks[:, :heads]
    v_ref[...] = jnp.dot(cb, wuv_ref[...], preferred_element_type=_F32).astype(v_ref.dtype)


def latent_kv(ckr, cs_tab, g_c, g_k, w_uk, w_uv, *, heads, tb):
    t, w = ckr.shape
    kv_lora = w_uk.shape[0]
    rope = (w - kv_lora) // 2
    nope = w_uk.shape[1] // heads
    g_kr = g_k[nope:]
    gkr2 = jnp.concatenate([g_kr, jnp.concatenate([g_kr[rope // 2:], g_kr[:rope // 2]])]).reshape(1, 2 * rope)
    full = lambda i: (0, 0)
    row = lambda i: (i, 0)
    return pl.pallas_call(
        functools.partial(_kv_body, heads=heads, qk_head=float(nope + rope)),
        grid=(t // tb,),
        in_specs=[
            pl.BlockSpec((tb, w), row),
            pl.BlockSpec((tb, 2 * rope), row),
            pl.BlockSpec((1, kv_lora), full),
            pl.BlockSpec((1, nope), full),
            pl.BlockSpec((1, 2 * rope), full),
            pl.BlockSpec(w_uk.shape, full),
            pl.BlockSpec(w_uv.shape, full),
        ],
        out_specs=[
            pl.BlockSpec((tb, kv_lora), row),
            pl.BlockSpec((tb, rope), row),
            pl.BlockSpec((tb, heads), row),
            pl.BlockSpec((tb, w_uk.shape[1]), row),
            pl.BlockSpec((tb, w_uv.shape[1]), row),
        ],
        out_shape=[
            jax.ShapeDtypeStruct((t, kv_lora), _F32),
            jax.ShapeDtypeStruct((t, rope), _F32),
            jax.ShapeDtypeStruct((t, heads), _F32),
            jax.ShapeDtypeStruct((t, w_uk.shape[1]), _BF),
            jax.ShapeDtypeStruct((t, w_uv.shape[1]), _BF),
        ],
        compiler_params=_params(("parallel",), 48),
        name="latent_kv",
    )(ckr, cs_tab, g_c.reshape(1, -1), g_k[:nope].reshape(1, -1), gkr2, w_uk, w_uv)


def _q_body(q_ref, c2_ref, s2_ref, gn_ref, gr_ref, grs_ref, qn_ref, qr_ref, *, heads, nope, rope):
    n_all = heads * nope
    r_all = heads * rope
    qk_head = float(nope + rope)
    lane = lax.broadcasted_iota(jnp.int32, (q_ref.shape[0], 2 * rope), 1)
    first = lane < rope
    for j in range(heads // 2):
        n0 = q_ref[:, (2 * j) * nope:(2 * j + 1) * nope]
        n1 = q_ref[:, (2 * j + 1) * nope:(2 * j + 2) * nope]
        r = q_ref[:, n_all + j * 2 * rope:n_all + (j + 1) * 2 * rope]
        rs = q_ref[:, n_all + r_all + j * 2 * rope:n_all + r_all + (j + 1) * 2 * rope]
        r2 = r * r
        ss0 = jnp.sum(n0 * n0, axis=-1, keepdims=True) + jnp.sum(jnp.where(first, r2, 0.0), axis=-1, keepdims=True)
        ss1 = jnp.sum(n1 * n1, axis=-1, keepdims=True) + jnp.sum(jnp.where(first, 0.0, r2), axis=-1, keepdims=True)
        inv0 = lax.rsqrt(ss0 / qk_head + EPS)
        inv1 = lax.rsqrt(ss1 / qk_head + EPS)
        qn_ref[:, (2 * j) * nope:(2 * j + 1) * nope] = n0 * inv0 * gn_ref[...]
        qn_ref[:, (2 * j + 1) * nope:(2 * j + 2) * nope] = n1 * inv1 * gn_ref[...]
        inv_r = jnp.where(first, inv0, inv1)
        qr_ref[:, j * 2 * rope:(j + 1) * 2 * rope] = (
            r * inv_r * gr_ref[...] * c2_ref[...] + rs * inv_r * grs_ref[...] * s2_ref[...])


def mla_q_finish(qraw, c2_tab, s2_tab, g_q, *, heads, nope, rope, tb):
    t, w = qraw.shape
    g_r = g_q[nope:]
    g_rs = jnp.concatenate([g_r[rope // 2:], g_r[:rope // 2]])
    row = lambda i: (i, 0)
    full = lambda i: (0, 0)
    return pl.pallas_call(
        functools.partial(_q_body, heads=heads, nope=nope, rope=rope),
        grid=(t // tb,),
        in_specs=[
            pl.BlockSpec((tb, w), row),
            pl.BlockSpec((tb, 2 * rope), row),
            pl.BlockSpec((tb, 2 * rope), row),
            pl.BlockSpec((1, nope), full),
            pl.BlockSpec((1, 2 * rope), full),
            pl.BlockSpec((1, 2 * rope), full),
        ],
        out_specs=[pl.BlockSpec((tb, heads * nope), row), pl.BlockSpec((tb, heads * rope), row)],
        out_shape=[jax.ShapeDtypeStruct((t, heads * nope), _F32), jax.ShapeDtypeStruct((t, heads * rope), _F32)],
        compiler_params=_params(("parallel",), 48),
        name="mla_q_finish",
    )(qraw, c2_tab, s2_tab, g_q[:nope].reshape(1, -1),
      jnp.tile(g_r, 2).reshape(1, -1), jnp.tile(g_rs, 2).reshape(1, -1))


def _flash_body(qn_ref, qr_ref, kn_ref, kr_ref, ks_ref, v_ref, o_ref, *, nope, rope, n_q_tiles):
    tq = qn_ref.shape[0]
    qi = pl.program_id(2)
    causal = (lax.broadcasted_iota(jnp.int32, (tq, tq), 1) <= lax.broadcasted_iota(jnp.int32, (tq, tq), 0))

    def tile(lo):
        for hh in range(2):
            hs = slice(hh * nope, (hh + 1) * nope)
            q = jnp.concatenate([qn_ref[:, hs], qr_ref[:, hh * rope:(hh + 1) * rope]], axis=1).astype(_BF)

            def scores(k0, k1, hh=hh, hs=hs, q=q):
                k = jnp.concatenate([kn_ref[k0:k1, hs], kr_ref[k0:k1, :]], axis=1)
                s = lax.dot_general(q, k, _NT, preferred_element_type=_F32)
                return s * ks_ref[0, hh:hh + 1, k0:k1]

            s_d = jnp.where(causal, scores(lo, lo + tq), -jnp.inf)
            m = jnp.max(s_d, axis=-1, keepdims=True)
            if lo:
                s_f = scores(0, lo)
                m = jnp.maximum(m, jnp.max(s_f, axis=-1, keepdims=True))
            p = jnp.exp2(s_d - m)
            l = jnp.sum(p, axis=-1, keepdims=True)
            acc = jnp.dot(p.astype(_BF), v_ref[lo:lo + tq, hs], preferred_element_type=_F32)
            if lo:
                p = jnp.exp2(s_f - m)
                l = l + jnp.sum(p, axis=-1, keepdims=True)
                acc = acc + jnp.dot(p.astype(_BF), v_ref[0:lo, hs], preferred_element_type=_F32)
            o_ref[:, hs] = (acc / l).astype(o_ref.dtype)

    for t in range(n_q_tiles):
        pl.when(qi == t)(functools.partial(tile, t * tq))


def mla_prompt_attention(qn, qr, kn, krope, ks_t, v, *, n_batch, seq, heads, nope, rope, tq):
    nq = seq // tq
    hp = heads // 2
    return pl.pallas_call(
        functools.partial(_flash_body, nope=nope, rope=rope, n_q_tiles=nq),
        grid=(n_batch, hp, nq),
        in_specs=[
            pl.BlockSpec((tq, 2 * nope), lambda n, h, i: (n * nq + i, h)),
            pl.BlockSpec((tq, 2 * rope), lambda n, h, i: (n * nq + i, h)),
            pl.BlockSpec((seq, 2 * nope), lambda n, h, i: (n, h)),
            pl.BlockSpec((seq, rope), lambda n, h, i: (n, 0)),
            pl.BlockSpec((1, 2, seq), lambda n, h, i: (n * hp + h, 0, 0)),
            pl.BlockSpec((seq, 2 * nope), lambda n, h, i: (n, h)),
        ],
        out_specs=pl.BlockSpec((tq, 2 * nope), lambda n, h, i: (n * nq + i, h)),
        out_shape=jax.ShapeDtypeStruct((n_batch * seq, heads * nope), _BF),
        compiler_params=_params(("parallel", "parallel", "arbitrary"), 40),
        name="mla_prompt_attention",
    )(qn, qr, kn, krope, ks_t, v)


def _head_nt_body(x_ref, g_ref, w_ref, o_ref):
    x = (x_ref[...] * g_ref[...]).astype(_BF)
    o_ref[...] = lax.dot_general(x, w_ref[...], _NT, preferred_element_type=_F32)


def absorb_q(qn_s, g_kn, w_uk):
    n = qn_s.shape[0]
    nope = g_kn.shape[0]
    kv_lora, w = w_uk.shape
    heads = w // nope
    return pl.pallas_call(
        _head_nt_body,
        grid=(heads,),
        in_specs=[
            pl.BlockSpec((n, nope), lambda h: (0, h)),
            pl.BlockSpec((1, nope), lambda h: (0, 0)),
            pl.BlockSpec((kv_lora, nope), lambda h: (0, h)),
        ],
        out_specs=pl.BlockSpec((n, kv_lora), lambda h: (0, h)),
        out_shape=jax.ShapeDtypeStruct((n, heads * kv_lora), _F32),
        compiler_params=_params(("parallel",), 40),
        name="absorb_q",
    )(qn_s, g_kn.reshape(1, nope), w_uk)


def _head_nn_body(x_ref, w_ref, o_ref):
    o_ref[...] = jnp.dot(x_ref[...].astype(_BF), w_ref[...], preferred_element_type=_F32).astype(o_ref.dtype)


def project_latent_out(o_lat, w_uv, heads):
    n = o_lat.shape[0]
    kv_lora = w_uv.shape[0]
    vh = w_uv.shape[1] // heads
    return pl.pallas_call(
        _head_nn_body,
        grid=(heads,),
        in_specs=[
            pl.BlockSpec((n, kv_lora), lambda h: (0, h)),
            pl.BlockSpec((kv_lora, vh), lambda h: (0, h)),
        ],
        out_specs=pl.BlockSpec((n, vh), lambda h: (0, h)),
        out_shape=jax.ShapeDtypeStruct((n, heads * vh), _BF),
        compiler_params=_params(("parallel",), 40),
        name="project_latent_out",
    )(o_lat, w_uv)


def _paged_body(pt_ref, qlat_ref, qr_ref, cnew_ref, krnew_ref, ksnew_ref, ckv_hbm, kr_hbm, ks_hbm,
                o_ref, c_buf, kr_buf, ks_buf, sem, *, pages_per_group, n_groups, scale):
    n = pl.program_id(0)
    n_samples = pl.num_programs(0)
    heads = qlat_ref.shape[1]
    page = c_buf.shape[2]
    keys = pages_per_group * page

    def group_copies(sample, g, slot):
        cps = []
        for p in range(pages_per_group):
            pg = pt_ref[sample, g * pages_per_group + p]
            cps.append(pltpu.make_async_copy(ckv_hbm.at[pg], c_buf.at[slot, p], sem.at[slot, 0]))
            cps.append(pltpu.make_async_copy(kr_hbm.at[pg], kr_buf.at[slot, p], sem.at[slot, 1]))
            cps.append(pltpu.make_async_copy(ks_hbm.at[pg], ks_buf.at[slot, p], sem.at[slot, 2]))
        return cps

    @pl.when(n == 0)
    def _():
        for cp in group_copies(0, 0, 0):
            cp.start()

    qlat32 = qlat_ref[0]
    qr32 = qr_ref[0]
    qlat = qlat32.astype(_BF)
    qr = qr32.astype(_BF)
    c_new = cnew_ref[0]
    s0 = (jnp.sum(qlat32 * c_new, axis=-1, keepdims=True)
          + jnp.sum(qr32 * krnew_ref[0], axis=-1, keepdims=True)) * ksnew_ref[0] * scale
    init = (s0, jnp.ones((heads, 1), _F32), jnp.broadcast_to(c_new, qlat32.shape))

    def group(g, carry):
        m, l, acc = carry
        slot = g % 2

        next_sample = n + (g + 1) // n_groups

        @pl.when(next_sample < n_samples)
        def _():
            for cp in group_copies(next_sample, (g + 1) % n_groups, 1 - slot):
                cp.start()

        for k, buf in enumerate((c_buf, kr_buf, ks_buf)):
            pltpu.make_async_copy(buf.at[slot], buf.at[slot], sem.at[slot, k]).wait()
        c = c_buf[slot].reshape(keys, c_buf.shape[3]).astype(_BF)
        pages = range(pages_per_group)
        s_r = jnp.concatenate([jnp.dot(qr, kr_buf[slot, p].astype(_BF), preferred_element_type=_F32)
                               for p in pages], axis=1)
        s = lax.dot_general(qlat, c, _NT, preferred_element_type=_F32) + s_r
        ks_t = jnp.concatenate([ks_buf[slot, p] for p in pages], axis=1)
        s = s * (ks_t * scale)
        m_new = jnp.maximum(m, jnp.max(s, axis=-1, keepdims=True))
        alpha = jnp.exp(m - m_new)
        p_exp = jnp.exp(s - m_new)
        l = l * alpha + jnp.sum(p_exp, axis=-1, keepdims=True)
        acc = acc * alpha + jnp.dot(p_exp.astype(_BF), c, preferred_element_type=_F32)
        return m_new, l, acc

    m, l, acc = lax.fori_loop(0, n_groups, group, init)
    o_ref[0] = acc / l


def paged_latent_attention(q_lat, q_r, c_new, kr_new, ks_new, cache_ckv, cache_krope_t, cache_kscale_t,
                           page_table, scale, pages_per_group):
    n, heads, kv_lora = q_lat.shape
    rope = q_r.shape[2]
    page = cache_ckv.shape[1]
    n_pages = page_table.shape[1]
    n_groups = n_pages // pages_per_group
    assert n_groups * pages_per_group == n_pages and n_groups % 2 == 0
    per = lambda i, pt: (i, 0, 0)
    grid_spec = pltpu.PrefetchScalarGridSpec(
        num_scalar_prefetch=1,
        grid=(n,),
        in_specs=[
            pl.BlockSpec((1, heads, kv_lora), per),
            pl.BlockSpec((1, heads, rope), per),
            pl.BlockSpec((1, 1, kv_lora), per),
            pl.BlockSpec((1, 1, rope), per),
            pl.BlockSpec((1, heads, 1), per),
            pl.BlockSpec(memory_space=pl.ANY),
            pl.BlockSpec(memory_space=pl.ANY),
            pl.BlockSpec(memory_space=pl.ANY),
        ],
        out_specs=pl.BlockSpec((1, heads, kv_lora), per),
        scratch_shapes=[
            pltpu.VMEM((2, pages_per_group, page, kv_lora), _F32),
            pltpu.VMEM((2, pages_per_group, rope, page), _F32),
            pltpu.VMEM((2, pages_per_group, heads, page), _F32),
            pltpu.SemaphoreType.DMA((2, 3)),
        ],
    )
    return pl.pallas_call(
        functools.partial(_paged_body, pages_per_group=pages_per_group, n_groups=n_groups, scale=scale),
        grid_spec=grid_spec,
        out_shape=jax.ShapeDtypeStruct((n, heads, kv_lora), _F32),
        compiler_params=_params(("arbitrary",), 40),
        name="paged_latent_attention",
    )(page_table, q_lat, q_r, c_new, kr_new, ks_new, cache_ckv, cache_krope_t, cache_kscale_t)


def _rope_tables(pos, rope):
    inv = 1.0 / (ROPE_THETA ** (jnp.arange(0, rope, 2, dtype=_F32) / rope))
    ang = pos.astype(_F32)[:, None] * inv[None, :]
    return jnp.cos(ang), jnp.sin(ang)


def kernel(x_prompt, x_sample, cache_ckv, cache_krope, cache_kscale, cache_mem_k, cache_mem_v, page_table, mem_prompt, g_mix, g_ffn, g_mem_in, w_mem_k, w_mem_v, g_mem_q, g_mem_k, w_out, peer_wq, peer_subkeys, peer_u, peer_v, a_w_in, a_g_v, a_w_s, a_b_s, b_w_in, b_g_cq, b_w_qb, b_g_q, kv_g_in, kv_w_dkv, kv_w_kr, kv_g_c, kv_g_k, kv_w_uk, kv_w_uv):
    n_batch, seq, d = x_prompt.shape
    n_dec = x_sample.shape[0]
    assert x_sample.shape[1] == 1
    depth = g_mix.shape[0]
    n_a = a_w_in.shape[0]
    mem_tokens, mem_heads, mem_hd = cache_mem_k.shape[2:]
    mem_w = mem_heads * mem_hd
    a_w = a_g_v.shape[1]
    groups, chunk = a_w_s.shape[1], a_w_s.shape[2]
    kv_lora, heads, nope = kv_w_uk.shape
    assert kv_w_uv.shape[2] == nope
    rope = kv_w_kr.shape[1]
    q_lora = b_g_cq.shape[1]
    n_p = n_batch * seq
    t = n_p + n_dec
    n_pages = page_table.shape[1]
    past = n_pages * cache_ckv.shape[1]
    tiles = {
        "norm": _tile(t, (416, 320, 128)),
        "mm": _tile(t, (832, 640, 128)),
        "peer": _tile(n_p, (512, 256, 128)),
        "experts": 512,
        "seq": _tile(seq, (512, 256, 128)),
        "pages": _tile(n_pages // 2, (16, 8, 4, 2, 1)),
    }

    x = jnp.concatenate([x_prompt.reshape(n_p, d), x_sample.reshape(n_dec, d)], axis=0)
    mem = mem_prompt.reshape(n_batch * mem_tokens, d)

    cos_p, sin_p = _rope_tables(jnp.arange(seq), rope)
    cos_s, sin_s = _rope_tables(past + jnp.arange(1), rope)
    cos = jnp.concatenate([jnp.tile(cos_p, (n_batch, 1)), jnp.tile(cos_s, (n_dec, 1))], axis=0)
    sin = jnp.concatenate([jnp.tile(sin_p, (n_batch, 1)), jnp.tile(sin_s, (n_dec, 1))], axis=0)
    cos2 = jnp.concatenate([cos, cos], axis=1)
    sin2 = jnp.concatenate([-sin, sin], axis=1)

    peer_u_b = peer_u.astype(_BF)
    peer_v_b = peer_v.astype(_BF)
    mem_k_new, mem_v_new, chunk_v_new = [], [], []
    kv_out = None
    for l in range(depth):
        hm = rmsnorm(mem, g_mem_in[l], _tile(mem.shape[0], (256, 128)))
        tmm = _tile(mem.shape[0], (512, 256, 128))
        mk = matmul([hm], [w_mem_k[l].astype(_BF)], tm=tmm, tn=512, out_dtype=_F32,
                    extras=[g_mem_k[l].reshape(1, mem_hd)],
                    epilogue=functools.partial(_headnorm_epilogue, width=mem_hd), name="mem_k")
        mv = matmul([hm], [w_mem_v[l].astype(_BF)], tm=tmm, tn=512, out_dtype=_F32, name="mem_v")
        mem_k_new.append(mk.reshape(n_batch, mem_tokens, mem_heads, mem_hd))
        mem_v_new.append(mv.reshape(n_batch, mem_tokens, mem_heads, mem_hd))

        if l == n_a:
            hk = rmsnorm(x, kv_g_in, tiles["norm"])
            kr_sw = jnp.concatenate([kv_w_kr[:, rope // 2:], kv_w_kr[:, :rope // 2]], axis=1)
            w_ckr = jnp.concatenate([kv_w_dkv, kv_w_kr, kr_sw], axis=1).astype(_BF)
            ckr = matmul([hk], [w_ckr], tm=tiles["mm"], tn=w_ckr.shape[1], out_dtype=_F32, name="kv_down")
            w_uk2 = kv_w_uk.reshape(kv_lora, heads * nope).astype(_BF)
            w_uv2 = kv_w_uv.reshape(kv_lora, -1).astype(_BF)
            c_all, krope_all, ks_all, kn_all, v_all = latent_kv(
                ckr, jnp.concatenate([cos2, sin2], axis=1), kv_g_c, kv_g_k, w_uk2, w_uv2,
                heads=heads, tb=tiles["norm"])
            kv_out = (c_all, krope_all, ks_all)

        h = rmsnorm(x, g_mix[l], tiles["norm"])
        if l < n_a:
            z = matmul([h], [a_w_in[l].astype(_BF)], tm=tiles["mm"], tn=512, out_dtype=_F32, name="a_in")
            tril = jnp.tril(a_w_s[l])
            diag = a_w_s[l][:, 0, 0][:, None, None] * jnp.eye(chunk, dtype=_F32)[None]
            ws_eff = jnp.stack([tril, diag])
            b_eff = jnp.stack([a_b_s[l].T, jnp.broadcast_to(a_b_s[l][:, 0][None, :], (chunk, groups))])
            mix, v_s = gating(z, a_g_v[l], ws_eff, b_eff, a_w, n_p // chunk)
            chunk_v_new.append(v_s.reshape(n_dec, 1, a_w))
            qm_col = 2 * a_w // mem_w
        else:
            b = l - n_a
            z = matmul([h], [b_w_in[b].astype(_BF)], tm=tiles["mm"], tn=512, out_dtype=_F32, name="b_in")
            cq = rmsnorm(z, b_g_cq[b], tiles["norm"])
            w3 = b_w_qb[b].reshape(q_lora, heads, nope + rope)
            w_r = w3[:, :, nope:]
            w_rs = jnp.concatenate([w_r[:, :, rope // 2:], w_r[:, :, :rope // 2]], axis=2)
            w_q = jnp.concatenate([w3[:, :, :nope].reshape(q_lora, heads * nope),
                                   w_r.reshape(q_lora, heads * rope),
                                   w_rs.reshape(q_lora, heads * rope)], axis=1).astype(_BF)
            qraw = matmul([cq], [w_q], tm=tiles["mm"], tn=512, out_dtype=_F32, name="q_up")
            qn, qr = mla_q_finish(qraw, jnp.tile(cos2, (1, 2)), jnp.tile(sin2, (1, 2)), b_g_q[b],
                                  heads=heads, nope=nope, rope=rope, tb=tiles["norm"])
            scale = float(nope + rope) ** -0.5
            ks_t = (ks_all[:n_p] * (scale * _LOG2E)).reshape(n_batch, seq, heads // 2, 2).transpose(0, 2, 3, 1)
            ks_t = ks_t.reshape(n_batch * (heads // 2), 2, seq)
            tq = tiles["seq"]
            mix_p = mla_prompt_attention(qn, qr, kn_all, krope_all.astype(_BF), ks_t, v_all,
                                         n_batch=n_batch, seq=seq, heads=heads, nope=nope, rope=rope,
                                         tq=tq)
            q_lat = absorb_q(qn[n_p:], kv_g_k[:nope], w_uk2)
            o_lat = paged_latent_attention(
                q_lat.reshape(n_dec, heads, kv_lora), qr[n_p:].reshape(n_dec, heads, rope),
                c_all[n_p:].reshape(n_dec, 1, kv_lora), krope_all[n_p:].reshape(n_dec, 1, rope),
                ks_all[n_p:].reshape(n_dec, heads, 1), cache_ckv,
                jnp.transpose(cache_krope, (0, 2, 1)), jnp.transpose(cache_kscale, (0, 2, 1)),
                page_table, scale, tiles["pages"])
            mix_s = project_latent_out(o_lat.reshape(n_dec, heads * kv_lora), w_uv2, heads)
            mix = jnp.concatenate([mix_p, mix_s], axis=0)
            qm_col = q_lora // mem_w

        mo_p = memattn_prompt(z, qm_col, mk.reshape(n_batch, mem_tokens, mem_w), mv.reshape(n_batch, mem_tokens, mem_w),
                              g_mem_q[l], n_batch=n_batch, seq=seq, tq=tiles["seq"], heads=mem_heads)
        qm_s = z[n_p:, qm_col * mem_w:(qm_col + 1) * mem_w].reshape(n_dec, mem_heads, mem_hd)
        mo_s = memattn_sample(qm_s, cache_mem_k, cache_mem_v, l, g_mem_q[l]).reshape(n_dec, mem_w)
        mo = jnp.concatenate([mo_p, mo_s.astype(_BF)], axis=0)
        w_o = w_out[l].astype(_BF)
        mix_w = mix.shape[1]
        x = matmul([mix, mo], [w_o[:mix_w], w_o[mix_w:]], tm=tiles["mm"], tn=512, out_dtype=_F32, res=x, name="w_out")
        x = peer_ffn(x, g_ffn[l], peer_wq[l], peer_subkeys[l], peer_u_b, peer_v_b, l, n_prompt=n_p, tiles=tiles)

    c_all, krope_all, ks_all = kv_out
    return (x[:n_p].reshape(n_batch, seq, d), x[n_p:].reshape(n_dec, 1, d),
            c_all[:n_p].reshape(n_batch, seq, kv_lora), krope_all[:n_p].reshape(n_batch, seq, rope),
            ks_all[:n_p].reshape(n_batch, seq, heads),
            jnp.stack(mem_k_new), jnp.stack(mem_v_new),
            c_all[n_p:].reshape(n_dec, 1, kv_lora), krope_all[n_p:].reshape(n_dec, 1, rope),
            ks_all[n_p:].reshape(n_dec, 1, heads),
            jnp.stack(chunk_v_new))
```

```python
import functools
import math

import jax
import jax.numpy as jnp
from jax import lax
from jax.experimental import pallas as pl
from jax.experimental.pallas import tpu as pltpu

EPS = 1e-6
ROPE_THETA = 10000.0
_BF = jnp.bfloat16
_F32 = jnp.float32
_MIB = 1024 * 1024
_SQRT_HALF = 0.7071067811865476
_LOG2E = 1.4426950408889634
_NT = (((1,), (1,)), ((), ()))


def _params(semantics, vmem_mib):
    return pltpu.CompilerParams(dimension_semantics=semantics, vmem_limit_bytes=vmem_mib * _MIB)


def _tile(n, prefs):
    for p in prefs:
        if n % p == 0:
            return p
    raise ValueError(f"no tile in {prefs} divides {n}")


def _gelu(x):
    return 0.5 * x * (1.0 + lax.erf(x * _SQRT_HALF))


def _rms(x, g):
    return x * lax.rsqrt(jnp.mean(x * x, axis=-1, keepdims=True) + EPS) * g


def _rmsnorm_body(x_ref, g_ref, o_ref):
    o_ref[...] = _rms(x_ref[...], g_ref[...]).astype(o_ref.dtype)


def rmsnorm(x, g, tm, col_block=0):
    m = x.shape[0]
    d = g.shape[0]
    return pl.pallas_call(
        _rmsnorm_body,
        grid=(m // tm,),
        in_specs=[pl.BlockSpec((tm, d), lambda i: (i, col_block)), pl.BlockSpec((1, d), lambda i: (0, 0))],
        out_specs=pl.BlockSpec((tm, d), lambda i: (i, 0)),
        out_shape=jax.ShapeDtypeStruct((m, d), _BF),
        compiler_params=_params(("parallel",), 40),
        name="rmsnorm",
    )(x, g.reshape(1, d))


def _mm_body(*refs, n_lhs, has_res, n_extra, epilogue):
    o_ref = refs[-1]
    acc = None
    for a_ref, w_ref in zip(refs[:n_lhs], refs[n_lhs:2 * n_lhs]):
        d = jnp.dot(a_ref[...], w_ref[...], preferred_element_type=_F32)
        acc = d if acc is None else acc + d
    pos = 2 * n_lhs
    if has_res:
        acc = acc + refs[pos][...]
        pos += 1
    if epilogue is not None:
        acc = epilogue(acc, *[r[...] for r in refs[pos:pos + n_extra]])
    o_ref[...] = acc.astype(o_ref.dtype)


def matmul(lhs, rhs, *, tm, tn, out_dtype, res=None, extras=(), epilogue=None, name="matmul"):
    m = lhs[0].shape[0]
    n = rhs[0].shape[1]
    in_specs = [pl.BlockSpec((tm, a.shape[1]), lambda i, j: (i, 0)) for a in lhs]
    in_specs += [pl.BlockSpec((w.shape[0], tn), lambda i, j: (0, j)) for w in rhs]
    args = list(lhs) + list(rhs)
    if res is not None:
        in_specs.append(pl.BlockSpec((tm, tn), lambda i, j: (i, j)))
        args.append(res)
    for e in extras:
        if e.shape[1] == n:
            in_specs.append(pl.BlockSpec((1, tn), lambda i, j: (0, j)))
        else:
            in_specs.append(pl.BlockSpec(e.shape, lambda i, j: (0, 0)))
        args.append(e)
    body = functools.partial(_mm_body, n_lhs=len(lhs), has_res=res is not None,
                             n_extra=len(extras), epilogue=epilogue)
    return pl.pallas_call(
        body,
        grid=(m // tm, n // tn),
        in_specs=in_specs,
        out_specs=pl.BlockSpec((tm, tn), lambda i, j: (i, j)),
        out_shape=jax.ShapeDtypeStruct((m, n), out_dtype),
        compiler_params=_params(("parallel", "arbitrary"), 48),
        name=name,
    )(*args)


def _headnorm_epilogue(acc, g, *, width):
    parts = [_rms(acc[:, c:c + width], g) for c in range(0, acc.shape[1], width)]
    return jnp.concatenate(parts, axis=-1)


def _gating_body(zu_ref, zv_ref, gv_ref, ws_ref, b_ref, mix_ref, v_ref, *, groups):
    u = _gelu(zu_ref[...])
    v = _rms(_gelu(zv_ref[...]), gv_ref[...])

    @pl.when(pl.program_id(0) == pl.num_programs(0) - 1)
    def _():
        v_ref[...] = v

    gd = v.shape[1] // groups
    b = b_ref[0]
    for g in range(groups):
        sl = slice(g * gd, (g + 1) * gd)
        mixed = jnp.dot(ws_ref[0, g].astype(_BF), v[:, sl].astype(_BF), preferred_element_type=_F32)
        mix_ref[:, sl] = (u[:, sl] * (mixed + b[:, g:g + 1])).astype(mix_ref.dtype)


def gating(z, g_v, ws_eff, b_eff, a_w, n_prompt_chunks):
    t = z.shape[0]
    _, groups, chunk, _ = ws_eff.shape
    n_chunks = t // chunk
    return pl.pallas_call(
        functools.partial(_gating_body, groups=groups),
        grid=(n_chunks,),
        in_specs=[
            pl.BlockSpec((chunk, a_w), lambda i: (i, 0)),
            pl.BlockSpec((chunk, a_w), lambda i: (i, 1)),
            pl.BlockSpec((1, a_w), lambda i: (0, 0)),
            pl.BlockSpec((1, groups, chunk, chunk), lambda i: (i // n_prompt_chunks, 0, 0, 0)),
            pl.BlockSpec((1, chunk, groups), lambda i: (i // n_prompt_chunks, 0, 0)),
        ],
        out_specs=[
            pl.BlockSpec((chunk, a_w), lambda i: (i, 0)),
            pl.BlockSpec((chunk, a_w), lambda i: (0, 0)),
        ],
        out_shape=[jax.ShapeDtypeStruct((t, a_w), _BF), jax.ShapeDtypeStruct((chunk, a_w), _F32)],
        compiler_params=_params(("arbitrary",), 40),
        name="gating",
    )(z, z, g_v.reshape(1, a_w), ws_eff, b_eff)


def _memattn_prompt_body(q_ref, k_ref, v_ref, g_ref, o_ref, *, heads):
    hd = q_ref.shape[1] // heads
    scale = hd ** -0.5
    for h in range(heads):
        sl = slice(h * hd, (h + 1) * hd)
        q = _rms(q_ref[:, sl], g_ref[...]).astype(_BF)
        s = lax.dot_general(q, k_ref[0, :, sl].astype(_BF), _NT, preferred_element_type=_F32) * scale
        e = jnp.exp(s - jnp.max(s, axis=-1, keepdims=True))
        p = (e / jnp.sum(e, axis=-1, keepdims=True)).astype(_BF)
        o_ref[:, sl] = jnp.dot(p, v_ref[0, :, sl].astype(_BF), preferred_element_type=_F32).astype(o_ref.dtype)


def memattn_prompt(z, col_block, mem_k, mem_v, g_q, *, n_batch, seq, tq, heads):
    mem_w = mem_k.shape[2]
    mt = mem_k.shape[1]
    nq = seq // tq
    return pl.pallas_call(
        functools.partial(_memattn_prompt_body, heads=heads),
        grid=(n_batch, nq),
        in_specs=[
            pl.BlockSpec((tq, mem_w), lambda n, i: (n * nq + i, col_block)),
            pl.BlockSpec((1, mt, mem_w), lambda n, i: (n, 0, 0)),
            pl.BlockSpec((1, mt, mem_w), lambda n, i: (n, 0, 0)),
            pl.BlockSpec((1, mem_w // heads), lambda n, i: (0, 0)),
        ],
        out_specs=pl.BlockSpec((tq, mem_w), lambda n, i: (n * nq + i, 0)),
        out_shape=jax.ShapeDtypeStruct((n_batch * seq, mem_w), _BF),
        compiler_params=_params(("parallel", "arbitrary"), 40),
        name="memattn_prompt",
    )(z, mem_k, mem_v, g_q.reshape(1, -1))


def _memattn_sample_body(q_ref, k_ref, v_ref, g_ref, o_ref):
    q = _rms(q_ref[0], g_ref[...])
    s = jnp.sum(k_ref[0, 0] * q[None], axis=-1, keepdims=True) * (q.shape[-1] ** -0.5)
    e = jnp.exp(s - jnp.max(s, axis=0, keepdims=True))
    p = e / jnp.sum(e, axis=0, keepdims=True)
    o_ref[0] = jnp.sum(p * v_ref[0, 0], axis=0)


def memattn_sample(qm, cache_k, cache_v, layer, g_q):
    _, n, mt, heads, hd = cache_k.shape
    return pl.pallas_call(
        _memattn_sample_body,
        grid=(n,),
        in_specs=[
            pl.BlockSpec((1, heads, hd), lambda i: (i, 0, 0)),
            pl.BlockSpec((1, 1, mt, heads, hd), lambda i: (layer, i, 0, 0, 0)),
            pl.BlockSpec((1, 1, mt, heads, hd), lambda i: (layer, i, 0, 0, 0)),
            pl.BlockSpec((1, hd), lambda i: (0, 0)),
        ],
        out_specs=pl.BlockSpec((1, heads, hd), lambda i: (i, 0, 0)),
        out_shape=jax.ShapeDtypeStruct((n, heads, hd), _F32),
        compiler_params=_params(("parallel",), 40),
        name="memattn_sample",
    )(qm, cache_k, cache_v, g_q.reshape(1, hd))


_TOPK = 16


def _topk16(x, on_pick):
    rows = x.shape[0]
    iota = lax.broadcasted_iota(jnp.int32, x.shape, 0).astype(_F32)
    for k in range(_TOPK):
        m = jnp.max(x, axis=0, keepdims=True)
        pos = jnp.min(jnp.where(x == m, iota, float(rows)), axis=0, keepdims=True)
        hit = iota == pos
        on_pick(k, m, hit)
        x = jnp.where(hit, -jnp.inf, x)


def _route_head(hh, q_ref, sk_ref, h1_ref, e1_ref, r2_ref, e2_ref):
    tb = q_ref.shape[0]
    nk, half = sk_ref.shape[1:]
    row16 = lax.broadcasted_iota(jnp.int32, (_TOPK, tb), 0).astype(_F32)
    row8 = lax.broadcasted_iota(jnp.int32, (8, tb), 0).astype(_F32)
    scores, ranks, tops, top_arr = [], [], [], []
    for h in range(2):
        col = (2 * hh + h) * half
        s = lax.dot_general(sk_ref[2 * hh + h], q_ref[:, col:col + half], _NT, preferred_element_type=_F32)
        state = {"rank": jnp.full((nk, tb), float(nk), _F32), "vals": [], "arr": jnp.zeros((_TOPK, tb), _F32)}

        def pick(k, m, hit, state=state):
            state["rank"] = jnp.where(hit, float(k), state["rank"])
            state["vals"].append(m)
            state["arr"] = jnp.where(row16 == float(k), m, state["arr"])

        _topk16(s, pick)
        scores.append(s)
        ranks.append(state["rank"])
        tops.append(state["vals"])
        top_arr.append(state["arr"])
    v2 = top_arr[1]
    blocks = [tops[0][0] + v2]
    for a in range(1, 8):
        blocks.append(jnp.where(row8 < float(_TOPK // (a + 1)), tops[0][a] + v2[:8], -jnp.inf))
    blocks.append(top_arr[0][8:] + tops[1][0])
    cand = jnp.concatenate(blocks, axis=0)
    best0 = tops[0][0] + tops[1][0]
    st = {"sel": jnp.zeros(cand.shape, _F32), "z": jnp.zeros((1, tb), _F32)}

    def pick2(k, m, hit):
        st["sel"] = jnp.where(hit, 1.0, st["sel"])
        st["z"] = st["z"] + jnp.exp(m - best0)

    _topk16(cand, pick2)
    sel = st["sel"]
    count = [jnp.sum(sel[:16], axis=0, keepdims=True)]
    count += [jnp.sum(sel[8 + 8 * a:16 + 8 * a], axis=0, keepdims=True) for a in range(1, 8)]
    count += [sel[72 + i:73 + i] for i in range(8)]
    h1 = jnp.zeros((nk, tb), _F32)
    for a in range(_TOPK):
        h1 = h1 + jnp.where(ranks[0] == float(a), count[a], 0.0)
    h1_ref[hh] = h1
    r2_ref[hh] = ranks[1].astype(r2_ref.dtype)
    e1_ref[hh] = jnp.exp(scores[0] - tops[0][0])
    e2_ref[hh] = (jnp.exp(scores[1] - tops[1][0]) / st["z"]).astype(e2_ref.dtype)


_ROUTE_HEADS_PER_STEP = 8


def _route_body(q_ref, sk_ref, h1_ref, e1_ref, r2_ref, e2_ref):
    for hh in range(h1_ref.shape[0]):
        _route_head(hh, q_ref, sk_ref, h1_ref, e1_ref, r2_ref, e2_ref)


def peer_route(q, subkeys, tb):
    t = q.shape[0]
    n2, nk, half = subkeys.shape
    heads = n2 // 2
    hp = _ROUTE_HEADS_PER_STEP
    out = jax.ShapeDtypeStruct((heads, nk, t), _F32)
    out_b = jax.ShapeDtypeStruct((heads, nk, t), _BF)
    ospec = pl.BlockSpec((hp, nk, tb), lambda i, p: (p, 0, i))
    return pl.pallas_call(
        _route_body,
        grid=(t // tb, heads // hp),
        in_specs=[
            pl.BlockSpec((tb, hp * 2 * half), lambda i, p: (i, p)),
            pl.BlockSpec((hp * 2, nk, half), lambda i, p: (p, 0, 0)),
        ],
        out_specs=[ospec, ospec, ospec, ospec],
        out_shape=[out, out, out_b, out_b],
        compiler_params=_params(("parallel", "arbitrary"), 40),
        name="peer_route",
    )(q, subkeys)


_EXPERT_CHUNK = 256


def _peer_body(hf_ref, u_ref, v_ref, h1_ref, e1_ref, r2_ref, e2_ref, x_ref, o_ref, *, heads, nk):
    j = pl.program_id(1)

    @pl.when(j == 0)
    def _():
        o_ref[...] = x_ref[...]

    te = u_ref.shape[0]
    ck = min(_EXPERT_CHUNK, te)
    hf = hf_ref[...]
    ats = [lax.dot_general(u_ref[c * ck:(c + 1) * ck, :], hf, _NT, preferred_element_type=_F32)
           for c in range(te // ck)]
    tm = hf.shape[0]
    pack = 16
    for c, at in enumerate(ats):
        pieces = []
        for s in range(ck // nk):
            i1 = j * (te // nk) + c * (ck // nk) + s
            g = None
            for p in range(heads):
                h1row = jnp.broadcast_to(h1_ref[p, pl.ds(i1, 1), :], (pack, tm)).astype(_BF)
                e1row = jnp.broadcast_to(e1_ref[p, pl.ds(i1, 1), :], (pack, tm)).astype(_BF)
                r2 = r2_ref[p].reshape(nk // pack, pack, tm)
                e2 = e2_ref[p].reshape(nk // pack, pack, tm)
                term = jnp.where(r2 < h1row[None], e1row[None] * e2, jnp.zeros((), _BF))
                g = term if g is None else g + term
            act = _gelu(at[s * nk:(s + 1) * nk, :]).astype(_BF).reshape(nk // pack, pack, tm)
            pieces.append((act * g).reshape(nk, tm))
        b = jnp.concatenate(pieces, axis=0)
        o_ref[...] += lax.dot_general(b, v_ref[c * ck:(c + 1) * ck, :], (((0,), (0,)), ((), ())),
                                      preferred_element_type=_F32)


def peer_dense(x, hf, u, v, layer, route, *, tm, te, row_block0, n_row_blocks):
    t, d = hf.shape
    n_exp = u.shape[1]
    heads, nk, _ = route[0].shape
    once = pl.Buffered(1)
    rspec = pl.BlockSpec((heads, nk, tm), lambda i, j: (0, 0, row_block0 + i), pipeline_mode=once)
    in_specs = [
        pl.BlockSpec((tm, d), lambda i, j: (row_block0 + i, 0)),
        pl.BlockSpec((None, te, d), lambda i, j: (layer, j, 0)),
        pl.BlockSpec((None, te, d), lambda i, j: (layer, j, 0)),
        rspec, rspec, rspec, rspec,
        pl.BlockSpec((tm, d), lambda i, j: (row_block0 + i, 0), pipeline_mode=once),
    ]
    args = [hf, u, v, *route, x]
    return pl.pallas_call(
        functools.partial(_peer_body, heads=heads, nk=nk),
        grid=(n_row_blocks, n_exp // te),
        in_specs=in_specs,
        out_specs=pl.BlockSpec((tm, d), lambda i, j: (row_block0 + i, 0), pipeline_mode=once),
        out_shape=jax.ShapeDtypeStruct((t, d), _F32),
        input_output_aliases={len(args) - 1: 0},
        compiler_params=_params(("parallel", "arbitrary"), 58),
        name="peer_dense",
    )(*args)


def peer_ffn(x, g_ffn, wq, subkeys, u_b, v_b, layer, *, n_prompt, tiles):
    t, d = x.shape
    heads, _, nk, half = subkeys.shape
    hf = rmsnorm(x, g_ffn, tiles["norm"])
    q = matmul([hf], [wq.astype(_BF)], tm=tiles["mm"], tn=512, out_dtype=_BF, name="peer_q")
    route = peer_route(q, subkeys.reshape(heads * 2, nk, half).astype(_BF), 128)
    tm = tiles["peer"]
    ts = t - n_prompt
    te = tiles["experts"]
    y = peer_dense(x, hf, u_b, v_b, layer, route, tm=tm, te=te, row_block0=0, n_row_blocks=n_prompt // tm)
    return peer_dense(y, hf, u_b, v_b, layer, route, tm=ts, te=te, row_block0=n_prompt // ts, n_row_blocks=1)


def _kv_body(ckr_ref, cs_ref, gc_ref, gkn_ref, gkr_ref, wuk_ref, wuv_ref,
             c_ref, krope_ref, ks_ref, kn_ref, v_ref, *, heads, qk_head):
    kv_lora = c_ref.shape[1]
    c = _rms(ckr_ref[:, :kv_lora], gc_ref[...])
    c_ref[...] = c
    cb = c.astype(_BF)
    blk = ckr_ref[:, kv_lora:]
    rope = blk.shape[1] // 2
    lane = lax.broadcasted_iota(jnp.int32, blk.shape, 1)
    ssq_r = jnp.sum(jnp.where(lane < rope, blk * blk, 0.0), axis=-1, keepdims=True)
    t = blk * gkr_ref[...] * cs_ref[...]
    krope_ref[...] = t[:, :rope] + t[:, rope:]
    kn = jnp.dot(cb, wuk_ref[...], preferred_element_type=_F32)
    nope = kn.shape[1] // heads
    lane_h = lax.broadcasted_iota(jnp.int32, (kn.shape[0], 128), 1)
    ks = jnp.zeros((kn.shape[0], 128), _F32)
    for h in range(heads):
        x = kn[:, h * nope:(h + 1) * nope]
        ms = (jnp.sum(x * x, axis=-1, keepdims=True) + ssq_r) / qk_head
        ks = jnp.where(lane_h == h, lax.rsqrt(ms + EPS), ks)
        kn_ref[:, h * nope:(h + 1) * nope] = (x * gkn_ref[...]).astype(kn_ref.dtype)
    ks_ref[...] = ks[:, :heads]
    v_ref[...] = jnp.dot(cb, wuv_ref[...], preferred_element_type=_F32).astype(v_ref.dtype)


def latent_kv(ckr, cs_tab, g_c, g_k, w_uk, w_uv, *, heads, tb):
    t, w = ckr.shape
    kv_lora = w_uk.shape[0]
    rope = (w - kv_lora) // 2
    nope = w_uk.shape[1] // heads
    g_kr = g_k[nope:]
    gkr2 = jnp.concatenate([g_kr, jnp.concatenate([g_kr[rope // 2:], g_kr[:rope // 2]])]).reshape(1, 2 * rope)
    full = lambda i: (0, 0)
    row = lambda i: (i, 0)
    return pl.pallas_call(
        functools.partial(_kv_body, heads=heads, qk_head=float(nope + rope)),
        grid=(t // tb,),
        in_specs=[
            pl.BlockSpec((tb, w), row),
            pl.BlockSpec((tb, 2 * rope), row),
            pl.BlockSpec((1, kv_lora), full),
            pl.BlockSpec((1, nope), full),
            pl.BlockSpec((1, 2 * rope), full),
            pl.BlockSpec(w_uk.shape, full),
            pl.BlockSpec(w_uv.shape, full),
        ],
        out_specs=[
            pl.BlockSpec((tb, kv_lora), row),
            pl.BlockSpec((tb, rope), row),
            pl.BlockSpec((tb, heads), row),
            pl.BlockSpec((tb, w_uk.shape[1]), row),
            pl.BlockSpec((tb, w_uv.shape[1]), row),
        ],
        out_shape=[
            jax.ShapeDtypeStruct((t, kv_lora), _F32),
            jax.ShapeDtypeStruct((t, rope), _F32),
            jax.ShapeDtypeStruct((t, heads), _F32),
            jax.ShapeDtypeStruct((t, w_uk.shape[1]), _BF),
            jax.ShapeDtypeStruct((t, w_uv.shape[1]), _BF),
        ],
        compiler_params=_params(("parallel",), 48),
        name="latent_kv",
    )(ckr, cs_tab, g_c.reshape(1, -1), g_k[:nope].reshape(1, -1), gkr2, w_uk, w_uv)


def _q_body(q_ref, c2_ref, s2_ref, gn_ref, gr_ref, grs_ref, qn_ref, qr_ref, *, heads, nope, rope):
    n_all = heads * nope
    r_all = heads * rope
    qk_head = float(nope + rope)
    lane = lax.broadcasted_iota(jnp.int32, (q_ref.shape[0], 2 * rope), 1)
    first = lane < rope
    for j in range(heads // 2):
        n0 = q_ref[:, (2 * j) * nope:(2 * j + 1) * nope]
        n1 = q_ref[:, (2 * j + 1) * nope:(2 * j + 2) * nope]
        r = q_ref[:, n_all + j * 2 * rope:n_all + (j + 1) * 2 * rope]
        rs = q_ref[:, n_all + r_all + j * 2 * rope:n_all + r_all + (j + 1) * 2 * rope]
        r2 = r * r
        ss0 = jnp.sum(n0 * n0, axis=-1, keepdims=True) + jnp.sum(jnp.where(first, r2, 0.0), axis=-1, keepdims=True)
        ss1 = jnp.sum(n1 * n1, axis=-1, keepdims=True) + jnp.sum(jnp.where(first, 0.0, r2), axis=-1, keepdims=True)
        inv0 = lax.rsqrt(ss0 / qk_head + EPS)
        inv1 = lax.rsqrt(ss1 / qk_head + EPS)
        qn_ref[:, (2 * j) * nope:(2 * j + 1) * nope] = n0 * inv0 * gn_ref[...]
        qn_ref[:, (2 * j + 1) * nope:(2 * j + 2) * nope] = n1 * inv1 * gn_ref[...]
        inv_r = jnp.where(first, inv0, inv1)
        qr_ref[:, j * 2 * rope:(j + 1) * 2 * rope] = (
            r * inv_r * gr_ref[...] * c2_ref[...] + rs * inv_r * grs_ref[...] * s2_ref[...])


def mla_q_finish(qraw, c2_tab, s2_tab, g_q, *, heads, nope, rope, tb):
    t, w = qraw.shape
    g_r = g_q[nope:]
    g_rs = jnp.concatenate([g_r[rope // 2:], g_r[:rope // 2]])
    row = lambda i: (i, 0)
    full = lambda i: (0, 0)
    return pl.pallas_call(
        functools.partial(_q_body, heads=heads, nope=nope, rope=rope),
        grid=(t // tb,),
        in_specs=[
            pl.BlockSpec((tb, w), row),
            pl.BlockSpec((tb, 2 * rope), row),
            pl.BlockSpec((tb, 2 * rope), row),
            pl.BlockSpec((1, nope), full),
            pl.BlockSpec((1, 2 * rope), full),
            pl.BlockSpec((1, 2 * rope), full),
        ],
        out_specs=[pl.BlockSpec((tb, heads * nope), row), pl.BlockSpec((tb, heads * rope), row)],
        out_shape=[jax.ShapeDtypeStruct((t, heads * nope), _F32), jax.ShapeDtypeStruct((t, heads * rope), _F32)],
        compiler_params=_params(("parallel",), 48),
        name="mla_q_finish",
    )(qraw, c2_tab, s2_tab, g_q[:nope].reshape(1, -1),
      jnp.tile(g_r, 2).reshape(1, -1), jnp.tile(g_rs, 2).reshape(1, -1))


def _flash_body(qn_ref, qr_ref, kn_ref, kr_ref, ks_ref, v_ref, o_ref, *, nope, rope, n_q_tiles):
    tq = qn_ref.shape[0]
    qi = pl.program_id(2)
    causal = (lax.broadcasted_iota(jnp.int32, (tq, tq), 1) <= lax.broadcasted_iota(jnp.int32, (tq, tq), 0))

    def tile(lo):
        for hh in range(2):
            hs = slice(hh * nope, (hh + 1) * nope)
            q = jnp.concatenate([qn_ref[:, hs], qr_ref[:, hh * rope:(hh + 1) * rope]], axis=1).astype(_BF)

            def scores(k0, k1, hh=hh, hs=hs, q=q):
                k = jnp.concatenate([kn_ref[k0:k1, hs], kr_ref[k0:k1, :]], axis=1)
                s = lax.dot_general(q, k, _NT, preferred_element_type=_F32)
                return s * ks_ref[0, hh:hh + 1, k0:k1]

            s_d = jnp.where(causal, scores(lo, lo + tq), -jnp.inf)
            m = jnp.max(s_d, axis=-1, keepdims=True)
            if lo:
                s_f = scores(0, lo)
                m = jnp.maximum(m, jnp.max(s_f, axis=-1, keepdims=True))
            p = jnp.exp2(s_d - m)
            l = jnp.sum(p, axis=-1, keepdims=True)
            acc = jnp.dot(p.astype(_BF), v_ref[lo:lo + tq, hs], preferred_element_type=_F32)
            if lo:
                p = jnp.exp2(s_f - m)
                l = l + jnp.sum(p, axis=-1, keepdims=True)
                acc = acc + jnp.dot(p.astype(_BF), v_ref[0:lo, hs], preferred_element_type=_F32)
            o_ref[:, hs] = (acc / l).astype(o_ref.dtype)

    for t in range(n_q_tiles):
        pl.when(qi == t)(functools.partial(tile, t * tq))


def mla_prompt_attention(qn, qr, kn, krope, ks_t, v, *, n_batch, seq, heads, nope, rope, tq):
    nq = seq // tq
    hp = heads // 2
    return pl.pallas_call(
        functools.partial(_flash_body, nope=nope, rope=rope, n_q_tiles=nq),
        grid=(n_batch, hp, nq),
        in_specs=[
            pl.BlockSpec((tq, 2 * nope), lambda n, h, i: (n * nq + i, h)),
            pl.BlockSpec((tq, 2 * rope), lambda n, h, i: (n * nq + i, h)),
            pl.BlockSpec((seq, 2 * nope), lambda n, h, i: (n, h)),
            pl.BlockSpec((seq, rope), lambda n, h, i: (n, 0)),
            pl.BlockSpec((1, 2, seq), lambda n, h, i: (n * hp + h, 0, 0)),
            pl.BlockSpec((seq, 2 * nope), lambda n, h, i: (n, h)),
        ],
        out_specs=pl.BlockSpec((tq, 2 * nope), lambda n, h, i: (n * nq + i, h)),
        out_shape=jax.ShapeDtypeStruct((n_batch * seq, heads * nope), _BF),
        compiler_params=_params(("parallel", "parallel", "arbitrary"), 40),
        name="mla_prompt_attention",
    )(qn, qr, kn, krope, ks_t, v)


def _head_nt_body(x_ref, g_ref, w_ref, o_ref):
    x = (x_ref[...] * g_ref[...]).astype(_BF)
    o_ref[...] = lax.dot_general(x, w_ref[...], _NT, preferred_element_type=_F32)


def absorb_q(qn_s, g_kn, w_uk):
    n = qn_s.shape[0]
    nope = g_kn.shape[0]
    kv_lora, w = w_uk.shape
    heads = w // nope
    return pl.pallas_call(
        _head_nt_body,
        grid=(heads,),
        in_specs=[
            pl.BlockSpec((n, nope), lambda h: (0, h)),
            pl.BlockSpec((1, nope), lambda h: (0, 0)),
            pl.BlockSpec((kv_lora, nope), lambda h: (0, h)),
        ],
        out_specs=pl.BlockSpec((n, kv_lora), lambda h: (0, h)),
        out_shape=jax.ShapeDtypeStruct((n, heads * kv_lora), _F32),
        compiler_params=_params(("parallel",), 40),
        name="absorb_q",
    )(qn_s, g_kn.reshape(1, nope), w_uk)


def _head_nn_body(x_ref, w_ref, o_ref):
    o_ref[...] = jnp.dot(x_ref[...].astype(_BF), w_ref[...], preferred_element_type=_F32).astype(o_ref.dtype)


def project_latent_out(o_lat, w_uv, heads):
    n = o_lat.shape[0]
    kv_lora = w_uv.shape[0]
    vh = w_uv.shape[1] // heads
    return pl.pallas_call(
        _head_nn_body,
        grid=(heads,),
        in_specs=[
            pl.BlockSpec((n, kv_lora), lambda h: (0, h)),
            pl.BlockSpec((kv_lora, vh), lambda h: (0, h)),
        ],
        out_specs=pl.BlockSpec((n, vh), lambda h: (0, h)),
        out_shape=jax.ShapeDtypeStruct((n, heads * vh), _BF),
        compiler_params=_params(("parallel",), 40),
        name="project_latent_out",
    )(o_lat, w_uv)


_PAGE_RING_SLOTS = 3

def _paged_body(pt_ref, qlat_ref, qr_ref, cnew_ref, krnew_ref, ksnew_ref, ckv_hbm, kr_hbm, ks_hbm,
                o_ref, c_buf, kr_buf, ks_buf, sem, *, pages_per_group, n_groups, scale):
    n = pl.program_id(0)
    n_samples = pl.num_programs(0)
    heads = qlat_ref.shape[1]
    slots = c_buf.shape[0]
    ahead = slots - 1
    page = c_buf.shape[2]
    keys = pages_per_group * page

    def group_copies(sample, g, slot):
        cps = []
        for p in range(pages_per_group):
            pg = pt_ref[sample, g * pages_per_group + p]
            cps.append(pltpu.make_async_copy(ckv_hbm.at[pg], c_buf.at[slot, p], sem.at[slot, 0]))
            cps.append(pltpu.make_async_copy(kr_hbm.at[pg], kr_buf.at[slot, p], sem.at[slot, 1]))
            cps.append(pltpu.make_async_copy(ks_hbm.at[pg], ks_buf.at[slot, p], sem.at[slot, 2]))
        return cps

    def start_group(w):
        sample = w // n_groups

        @pl.when(sample < n_samples)
        def _():
            for cp in group_copies(sample, w % n_groups, w % slots):
                cp.start()

    @pl.when(n == 0)
    def _():
        for w in range(ahead):
            start_group(w)

    qlat32 = qlat_ref[0]
    qr32 = qr_ref[0]
    qlat = qlat32.astype(_BF)
    qr = qr32.astype(_BF)
    c_new = cnew_ref[0]
    s0 = (jnp.sum(qlat32 * c_new, axis=-1, keepdims=True)
          + jnp.sum(qr32 * krnew_ref[0], axis=-1, keepdims=True)) * ksnew_ref[0] * scale
    init = (s0, jnp.ones((heads, 1), _F32), jnp.broadcast_to(c_new, qlat32.shape))

    def group(g, carry):
        m, l, acc = carry
        w = n * n_groups + g
        slot = w % slots
        start_group(w + ahead)

        for k, buf in enumerate((c_buf, kr_buf, ks_buf)):
            pltpu.make_async_copy(buf.at[slot], buf.at[slot], sem.at[slot, k]).wait()
        c = c_buf[slot].reshape(keys, c_buf.shape[3]).astype(_BF)
        pages = range(pages_per_group)
        s_r = jnp.concatenate([jnp.dot(qr, kr_buf[slot, p].astype(_BF), preferred_element_type=_F32)
                               for p in pages], axis=1)
        s = lax.dot_general(qlat, c, _NT, preferred_element_type=_F32) + s_r
        ks_t = jnp.concatenate([ks_buf[slot, p] for p in pages], axis=1)
        s = s * (ks_t * scale)
        m_new = jnp.maximum(m, jnp.max(s, axis=-1, keepdims=True))
        alpha = jnp.exp(m - m_new)
        p_exp = jnp.exp(s - m_new)
        l = l * alpha + jnp.sum(p_exp, axis=-1, keepdims=True)
        acc = acc * alpha + jnp.dot(p_exp.astype(_BF), c, preferred_element_type=_F32)
        return m_new, l, acc

    m, l, acc = lax.fori_loop(0, n_groups, group, init)
    o_ref[0] = acc / l


def paged_latent_attention(q_lat, q_r, c_new, kr_new, ks_new, cache_ckv, cache_krope_t, cache_kscale_t,
                           page_table, scale, pages_per_group):
    n, heads, kv_lora = q_lat.shape
    rope = q_r.shape[2]
    page = cache_ckv.shape[1]
    n_pages = page_table.shape[1]
    n_groups = n_pages // pages_per_group
    assert n_groups * pages_per_group == n_pages
    slots = _PAGE_RING_SLOTS
    per = lambda i, pt: (i, 0, 0)
    grid_spec = pltpu.PrefetchScalarGridSpec(
        num_scalar_prefetch=1,
        grid=(n,),
        in_specs=[
            pl.BlockSpec((1, heads, kv_lora), per),
            pl.BlockSpec((1, heads, rope), per),
            pl.BlockSpec((1, 1, kv_lora), per),
            pl.BlockSpec((1, 1, rope), per),
            pl.BlockSpec((1, heads, 1), per),
            pl.BlockSpec(memory_space=pl.ANY),
            pl.BlockSpec(memory_space=pl.ANY),
            pl.BlockSpec(memory_space=pl.ANY),
        ],
        out_specs=pl.BlockSpec((1, heads, kv_lora), per),
        scratch_shapes=[
            pltpu.VMEM((slots, pages_per_group, page, kv_lora), _F32),
            pltpu.VMEM((slots, pages_per_group, rope, page), _F32),
            pltpu.VMEM((slots, pages_per_group, heads, page), _F32),
            pltpu.SemaphoreType.DMA((slots, 3)),
        ],
    )
    return pl.pallas_call(
        functools.partial(_paged_body, pages_per_group=pages_per_group, n_groups=n_groups, scale=scale),
        grid_spec=grid_spec,
        out_shape=jax.ShapeDtypeStruct((n, heads, kv_lora), _F32),
        compiler_params=_params(("arbitrary",), 40),
        name="paged_latent_attention",
    )(page_table, q_lat, q_r, c_new, kr_new, ks_new, cache_ckv, cache_krope_t, cache_kscale_t)


def _rope_tables(pos, rope):
    inv = 1.0 / (ROPE_THETA ** (jnp.arange(0, rope, 2, dtype=_F32) / rope))
    ang = pos.astype(_F32)[:, None] * inv[None, :]
    return jnp.cos(ang), jnp.sin(ang)


def kernel(x_prompt, x_sample, cache_ckv, cache_krope, cache_kscale, cache_mem_k, cache_mem_v, page_table, mem_prompt, g_mix, g_ffn, g_mem_in, w_mem_k, w_mem_v, g_mem_q, g_mem_k, w_out, peer_wq, peer_subkeys, peer_u, peer_v, a_w_in, a_g_v, a_w_s, a_b_s, b_w_in, b_g_cq, b_w_qb, b_g_q, kv_g_in, kv_w_dkv, kv_w_kr, kv_g_c, kv_g_k, kv_w_uk, kv_w_uv):
    n_batch, seq, d = x_prompt.shape
    n_dec = x_sample.shape[0]
    assert x_sample.shape[1] == 1
    depth = g_mix.shape[0]
    n_a = a_w_in.shape[0]
    mem_tokens, mem_heads, mem_hd = cache_mem_k.shape[2:]
    mem_w = mem_heads * mem_hd
    a_w = a_g_v.shape[1]
    groups, chunk = a_w_s.shape[1], a_w_s.shape[2]
    kv_lora, heads, nope = kv_w_uk.shape
    assert kv_w_uv.shape[2] == nope
    rope = kv_w_kr.shape[1]
    q_lora = b_g_cq.shape[1]
    n_p = n_batch * seq
    t = n_p + n_dec
    n_pages = page_table.shape[1]
    past = n_pages * cache_ckv.shape[1]
    tiles = {
        "norm": _tile(t, (416, 320, 128)),
        "mm": _tile(t, (832, 640, 128)),
        "peer": _tile(n_p, (512, 256, 128)),
        "experts": 512,
        "seq": _tile(seq, (512, 256, 128)),
        "pages": _tile(n_pages, (16, 8, 4, 2, 1)),
    }

    x = jnp.concatenate([x_prompt.reshape(n_p, d), x_sample.reshape(n_dec, d)], axis=0)
    mem = mem_prompt.reshape(n_batch * mem_tokens, d)

    cos_p, sin_p = _rope_tables(jnp.arange(seq), rope)
    cos_s, sin_s = _rope_tables(past + jnp.arange(1), rope)
    cos = jnp.concatenate([jnp.tile(cos_p, (n_batch, 1)), jnp.tile(cos_s, (n_dec, 1))], axis=0)
    sin = jnp.concatenate([jnp.tile(sin_p, (n_batch, 1)), jnp.tile(sin_s, (n_dec, 1))], axis=0)
    cos2 = jnp.concatenate([cos, cos], axis=1)
    sin2 = jnp.concatenate([-sin, sin], axis=1)

    peer_u_b = peer_u.astype(_BF)
    peer_v_b = peer_v.astype(_BF)
    mem_k_new, mem_v_new, chunk_v_new = [], [], []
    kv_out = None
    for l in range(depth):
        hm = rmsnorm(mem, g_mem_in[l], _tile(mem.shape[0], (256, 128)))
        tmm = _tile(mem.shape[0], (512, 256, 128))
        mk = matmul([hm], [w_mem_k[l].astype(_BF)], tm=tmm, tn=512, out_dtype=_F32,
                    extras=[g_mem_k[l].reshape(1, mem_hd)],
                    epilogue=functools.partial(_headnorm_epilogue, width=mem_hd), name="mem_k")
        mv = matmul([hm], [w_mem_v[l].astype(_BF)], tm=tmm, tn=512, out_dtype=_F32, name="mem_v")
        mem_k_new.append(mk.reshape(n_batch, mem_tokens, mem_heads, mem_hd))
        mem_v_new.append(mv.reshape(n_batch, mem_tokens, mem_heads, mem_hd))

        if l == n_a:
            hk = rmsnorm(x, kv_g_in, tiles["norm"])
            kr_sw = jnp.concatenate([kv_w_kr[:, rope // 2:], kv_w_kr[:, :rope // 2]], axis=1)
            w_ckr = jnp.concatenate([kv_w_dkv, kv_w_kr, kr_sw], axis=1).astype(_BF)
            ckr = matmul([hk], [w_ckr], tm=tiles["mm"], tn=w_ckr.shape[1], out_dtype=_F32, name="kv_down")
            w_uk2 = kv_w_uk.reshape(kv_lora, heads * nope).astype(_BF)
            w_uv2 = kv_w_uv.reshape(kv_lora, -1).astype(_BF)
            c_all, krope_all, ks_all, kn_all, v_all = latent_kv(
                ckr, jnp.concatenate([cos2, sin2], axis=1), kv_g_c, kv_g_k, w_uk2, w_uv2,
                heads=heads, tb=tiles["norm"])
            kv_out = (c_all, krope_all, ks_all)

        h = rmsnorm(x, g_mix[l], tiles["norm"])
        if l < n_a:
            z = matmul([h], [a_w_in[l].astype(_BF)], tm=tiles["mm"], tn=512, out_dtype=_F32, name="a_in")
            tril = jnp.tril(a_w_s[l])
            diag = a_w_s[l][:, 0, 0][:, None, None] * jnp.eye(chunk, dtype=_F32)[None]
            ws_eff = jnp.stack([tril, diag])
            b_eff = jnp.stack([a_b_s[l].T, jnp.broadcast_to(a_b_s[l][:, 0][None, :], (chunk, groups))])
            mix, v_s = gating(z, a_g_v[l], ws_eff, b_eff, a_w, n_p // chunk)
            chunk_v_new.append(v_s.reshape(n_dec, 1, a_w))
            qm_col = 2 * a_w // mem_w
        else:
            b = l - n_a
            z = matmul([h], [b_w_in[b].astype(_BF)], tm=tiles["mm"], tn=512, out_dtype=_F32, name="b_in")
            cq = rmsnorm(z, b_g_cq[b], tiles["norm"])
            w3 = b_w_qb[b].reshape(q_lora, heads, nope + rope)
            w_r = w3[:, :, nope:]
            w_rs = jnp.concatenate([w_r[:, :, rope // 2:], w_r[:, :, :rope // 2]], axis=2)
            w_q = jnp.concatenate([w3[:, :, :nope].reshape(q_lora, heads * nope),
                                   w_r.reshape(q_lora, heads * rope),
                                   w_rs.reshape(q_lora, heads * rope)], axis=1).astype(_BF)
            qraw = matmul([cq], [w_q], tm=tiles["mm"], tn=512, out_dtype=_F32, name="q_up")
            qn, qr = mla_q_finish(qraw, jnp.tile(cos2, (1, 2)), jnp.tile(sin2, (1, 2)), b_g_q[b],
                                  heads=heads, nope=nope, rope=rope, tb=tiles["norm"])
            scale = float(nope + rope) ** -0.5
            ks_t = (ks_all[:n_p] * (scale * _LOG2E)).reshape(n_batch, seq, heads // 2, 2).transpose(0, 2, 3, 1)
            ks_t = ks_t.reshape(n_batch * (heads // 2), 2, seq)
            tq = tiles["seq"]
            mix_p = mla_prompt_attention(qn, qr, kn_all, krope_all.astype(_BF), ks_t, v_all,
                                         n_batch=n_batch, seq=seq, heads=heads, nope=nope, rope=rope,
                                         tq=tq)
            q_lat = absorb_q(qn[n_p:], kv_g_k[:nope], w_uk2)
            o_lat = paged_latent_attention(
                q_lat.reshape(n_dec, heads, kv_lora), qr[n_p:].reshape(n_dec, heads, rope),
                c_all[n_p:].reshape(n_dec, 1, kv_lora), krope_all[n_p:].reshape(n_dec, 1, rope),
                ks_all[n_p:].reshape(n_dec, heads, 1), cache_ckv,
                jnp.transpose(cache_krope, (0, 2, 1)), jnp.transpose(cache_kscale, (0, 2, 1)),
                page_table, scale, tiles["pages"])
            mix_s = project_latent_out(o_lat.reshape(n_dec, heads * kv_lora), w_uv2, heads)
            mix = jnp.concatenate([mix_p, mix_s], axis=0)
            qm_col = q_lora // mem_w

        mo_p = memattn_prompt(z, qm_col, mk.reshape(n_batch, mem_tokens, mem_w), mv.reshape(n_batch, mem_tokens, mem_w),
                              g_mem_q[l], n_batch=n_batch, seq=seq, tq=tiles["seq"], heads=mem_heads)
        qm_s = z[n_p:, qm_col * mem_w:(qm_col + 1) * mem_w].reshape(n_dec, mem_heads, mem_hd)
        mo_s = memattn_sample(qm_s, cache_mem_k, cache_mem_v, l, g_mem_q[l]).reshape(n_dec, mem_w)
        mo = jnp.concatenate([mo_p, mo_s.astype(_BF)], axis=0)
        w_o = w_out[l].astype(_BF)
        mix_w = mix.shape[1]
        x = matmul([mix, mo], [w_o[:mix_w], w_o[mix_w:]], tm=tiles["mm"], tn=512, out_dtype=_F32, res=x, name="w_out")
        x = peer_ffn(x, g_ffn[l], peer_wq[l], peer_subkeys[l], peer_u_b, peer_v_b, l, n_prompt=n_p, tiles=tiles)

    c_all, krope_all, ks_all = kv_out
    return (x[:n_p].reshape(n_batch, seq, d), x[n_p:].reshape(n_dec, 1, d),
            c_all[:n_p].reshape(n_batch, seq, kv_lora), krope_all[:n_p].reshape(n_batch, seq, rope),
            ks_all[:n_p].reshape(n_batch, seq, heads),
            jnp.stack(mem_k_new), jnp.stack(mem_v_new),
            c_all[n_p:].reshape(n_dec, 1, kv_lora), krope_all[n_p:].reshape(n_dec, 1, rope),
            ks_all[n_p:].reshape(n_dec, 1, heads),
            jnp.stack(chunk_v_new))
```

```python
import functools
import math

import jax
import jax.numpy as jnp
from jax import lax
from jax.experimental import pallas as pl
from jax.experimental.pallas import tpu as pltpu

EPS = 1e-6
ROPE_THETA = 10000.0
_BF = jnp.bfloat16
_F32 = jnp.float32
_MIB = 1024 * 1024
_SQRT_HALF = 0.7071067811865476
_LOG2E = 1.4426950408889634
_NT = (((1,), (1,)), ((), ()))


def _params(semantics, vmem_mib):
    return pltpu.CompilerParams(dimension_semantics=semantics, vmem_limit_bytes=vmem_mib * _MIB)


def _tile(n, prefs):
    for p in prefs:
        if n % p == 0:
            return p
    raise ValueError(f"no tile in {prefs} divides {n}")


def _gelu(x):
    return 0.5 * x * (1.0 + lax.erf(x * _SQRT_HALF))


def _rms(x, g):
    return x * lax.rsqrt(jnp.mean(x * x, axis=-1, keepdims=True) + EPS) * g


def _rmsnorm_body(x_ref, g_ref, o_ref):
    o_ref[...] = _rms(x_ref[...], g_ref[...]).astype(o_ref.dtype)


def rmsnorm(x, g, tm, col_block=0):
    m = x.shape[0]
    d = g.shape[0]
    return pl.pallas_call(
        _rmsnorm_body,
        grid=(m // tm,),
        in_specs=[pl.BlockSpec((tm, d), lambda i: (i, col_block)), pl.BlockSpec((1, d), lambda i: (0, 0))],
        out_specs=pl.BlockSpec((tm, d), lambda i: (i, 0)),
        out_shape=jax.ShapeDtypeStruct((m, d), _BF),
        compiler_params=_params(("parallel",), 40),
        name="rmsnorm",
    )(x, g.reshape(1, d))


def _mm_body(*refs, n_lhs, has_res, n_extra, epilogue):
    o_ref = refs[-1]
    acc = None
    for a_ref, w_ref in zip(refs[:n_lhs], refs[n_lhs:2 * n_lhs]):
        d = jnp.dot(a_ref[...], w_ref[...], preferred_element_type=_F32)
        acc = d if acc is None else acc + d
    pos = 2 * n_lhs
    if has_res:
        acc = acc + refs[pos][...]
        pos += 1
    if epilogue is not None:
        acc = epilogue(acc, *[r[...] for r in refs[pos:pos + n_extra]])
    o_ref[...] = acc.astype(o_ref.dtype)


def matmul(lhs, rhs, *, tm, tn, out_dtype, res=None, extras=(), epilogue=None, name="matmul"):
    m = lhs[0].shape[0]
    n = rhs[0].shape[1]
    in_specs = [pl.BlockSpec((tm, a.shape[1]), lambda i, j: (i, 0)) for a in lhs]
    in_specs += [pl.BlockSpec((w.shape[0], tn), lambda i, j: (0, j)) for w in rhs]
    args = list(lhs) + list(rhs)
    if res is not None:
        in_specs.append(pl.BlockSpec((tm, tn), lambda i, j: (i, j)))
        args.append(res)
    for e in extras:
        if e.shape[1] == n:
            in_specs.append(pl.BlockSpec((1, tn), lambda i, j: (0, j)))
        else:
            in_specs.append(pl.BlockSpec(e.shape, lambda i, j: (0, 0)))
        args.append(e)
    body = functools.partial(_mm_body, n_lhs=len(lhs), has_res=res is not None,
                             n_extra=len(extras), epilogue=epilogue)
    return pl.pallas_call(
        body,
        grid=(m // tm, n // tn),
        in_specs=in_specs,
        out_specs=pl.BlockSpec((tm, tn), lambda i, j: (i, j)),
        out_shape=jax.ShapeDtypeStruct((m, n), out_dtype),
        compiler_params=_params(("parallel", "arbitrary"), 48),
        name=name,
    )(*args)


def _headnorm_epilogue(acc, g, *, width):
    parts = [_rms(acc[:, c:c + width], g) for c in range(0, acc.shape[1], width)]
    return jnp.concatenate(parts, axis=-1)


def _gating_body(zu_ref, zv_ref, gv_ref, ws_ref, b_ref, mix_ref, v_ref, *, groups):
    u = _gelu(zu_ref[...])
    v = _rms(_gelu(zv_ref[...]), gv_ref[...])

    @pl.when(pl.program_id(0) == pl.num_programs(0) - 1)
    def _():
        v_ref[...] = v

    gd = v.shape[1] // groups
    b = b_ref[0]
    for g in range(groups):
        sl = slice(g * gd, (g + 1) * gd)
        mixed = jnp.dot(ws_ref[0, g].astype(_BF), v[:, sl].astype(_BF), preferred_element_type=_F32)
        mix_ref[:, sl] = (u[:, sl] * (mixed + b[:, g:g + 1])).astype(mix_ref.dtype)


def gating(z, g_v, ws_eff, b_eff, a_w, n_prompt_chunks):
    t = z.shape[0]
    _, groups, chunk, _ = ws_eff.shape
    n_chunks = t // chunk
    return pl.pallas_call(
        functools.partial(_gating_body, groups=groups),
        grid=(n_chunks,),
        in_specs=[
            pl.BlockSpec((chunk, a_w), lambda i: (i, 0)),
            pl.BlockSpec((chunk, a_w), lambda i: (i, 1)),
            pl.BlockSpec((1, a_w), lambda i: (0, 0)),
            pl.BlockSpec((1, groups, chunk, chunk), lambda i: (i // n_prompt_chunks, 0, 0, 0)),
            pl.BlockSpec((1, chunk, groups), lambda i: (i // n_prompt_chunks, 0, 0)),
        ],
        out_specs=[
            pl.BlockSpec((chunk, a_w), lambda i: (i, 0)),
            pl.BlockSpec((chunk, a_w), lambda i: (0, 0)),
        ],
        out_shape=[jax.ShapeDtypeStruct((t, a_w), _BF), jax.ShapeDtypeStruct((chunk, a_w), _F32)],
        compiler_params=_params(("arbitrary",), 40),
        name="gating",
    )(z, z, g_v.reshape(1, a_w), ws_eff, b_eff)


def _memattn_prompt_body(q_ref, k_ref, v_ref, g_ref, o_ref, *, heads):
    hd = q_ref.shape[1] // heads
    scale = hd ** -0.5
    for h in range(heads):
        sl = slice(h * hd, (h + 1) * hd)
        q = _rms(q_ref[:, sl], g_ref[...]).astype(_BF)
        s = lax.dot_general(q, k_ref[0, :, sl].astype(_BF), _NT, preferred_element_type=_F32) * scale
        e = jnp.exp(s - jnp.max(s, axis=-1, keepdims=True))
        p = (e / jnp.sum(e, axis=-1, keepdims=True)).astype(_BF)
        o_ref[:, sl] = jnp.dot(p, v_ref[0, :, sl].astype(_BF), preferred_element_type=_F32).astype(o_ref.dtype)


def memattn_prompt(z, col_block, mem_k, mem_v, g_q, *, n_batch, seq, tq, heads):
    mem_w = mem_k.shape[2]
    mt = mem_k.shape[1]
    nq = seq // tq
    return pl.pallas_call(
        functools.partial(_memattn_prompt_body, heads=heads),
        grid=(n_batch, nq),
        in_specs=[
            pl.BlockSpec((tq, mem_w), lambda n, i: (n * nq + i, col_block)),
            pl.BlockSpec((1, mt, mem_w), lambda n, i: (n, 0, 0)),
            pl.BlockSpec((1, mt, mem_w), lambda n, i: (n, 0, 0)),
            pl.BlockSpec((1, mem_w // heads), lambda n, i: (0, 0)),
        ],
        out_specs=pl.BlockSpec((tq, mem_w), lambda n, i: (n * nq + i, 0)),
        out_shape=jax.ShapeDtypeStruct((n_batch * seq, mem_w), _BF),
        compiler_params=_params(("parallel", "arbitrary"), 40),
        name="memattn_prompt",
    )(z, mem_k, mem_v, g_q.reshape(1, -1))


def _memattn_sample_body(q_ref, k_ref, v_ref, g_ref, o_ref):
    q = _rms(q_ref[0], g_ref[...])
    s = jnp.sum(k_ref[0, 0] * q[None], axis=-1, keepdims=True) * (q.shape[-1] ** -0.5)
    e = jnp.exp(s - jnp.max(s, axis=0, keepdims=True))
    p = e / jnp.sum(e, axis=0, keepdims=True)
    o_ref[0] = jnp.sum(p * v_ref[0, 0], axis=0)


def memattn_sample(qm, cache_k, cache_v, layer, g_q):
    _, n, mt, heads, hd = cache_k.shape
    return pl.pallas_call(
        _memattn_sample_body,
        grid=(n,),
        in_specs=[
            pl.BlockSpec((1, heads, hd), lambda i: (i, 0, 0)),
            pl.BlockSpec((1, 1, mt, heads, hd), lambda i: (layer, i, 0, 0, 0)),
            pl.BlockSpec((1, 1, mt, heads, hd), lambda i: (layer, i, 0, 0, 0)),
            pl.BlockSpec((1, hd), lambda i: (0, 0)),
        ],
        out_specs=pl.BlockSpec((1, heads, hd), lambda i: (i, 0, 0)),
        out_shape=jax.ShapeDtypeStruct((n, heads, hd), _F32),
        compiler_params=_params(("parallel",), 40),
        name="memattn_sample",
    )(qm, cache_k, cache_v, g_q.reshape(1, hd))


_TOPK = 16


def _topk16(x, on_pick):
    rows = x.shape[0]
    iota = lax.broadcasted_iota(jnp.int32, x.shape, 0).astype(_F32)
    for k in range(_TOPK):
        m = jnp.max(x, axis=0, keepdims=True)
        pos = jnp.min(jnp.where(x == m, iota, float(rows)), axis=0, keepdims=True)
        hit = iota == pos
        on_pick(k, m, hit)
        x = jnp.where(hit, -jnp.inf, x)


def _route_head(hh, q_ref, sk_ref, h1_ref, e1_ref, r2_ref, e2_ref):
    tb = q_ref.shape[0]
    nk, half = sk_ref.shape[1:]
    row16 = lax.broadcasted_iota(jnp.int32, (_TOPK, tb), 0).astype(_F32)
    row8 = lax.broadcasted_iota(jnp.int32, (8, tb), 0).astype(_F32)
    scores, ranks, tops, top_arr = [], [], [], []
    for h in range(2):
        col = (2 * hh + h) * half
        s = lax.dot_general(sk_ref[2 * hh + h], q_ref[:, col:col + half], _NT, preferred_element_type=_F32)
        state = {"rank": jnp.full((nk, tb), float(nk), _F32), "vals": [], "arr": jnp.zeros((_TOPK, tb), _F32)}

        def pick(k, m, hit, state=state):
            state["rank"] = jnp.where(hit, float(k), state["rank"])
            state["vals"].append(m)
            state["arr"] = jnp.where(row16 == float(k), m, state["arr"])

        _topk16(s, pick)
        scores.append(s)
        ranks.append(state["rank"])
        tops.append(state["vals"])
        top_arr.append(state["arr"])
    v2 = top_arr[1]
    blocks = [tops[0][0] + v2]
    for a in range(1, 8):
        blocks.append(jnp.where(row8 < float(_TOPK // (a + 1)), tops[0][a] + v2[:8], -jnp.inf))
    blocks.append(top_arr[0][8:] + tops[1][0])
    cand = jnp.concatenate(blocks, axis=0)
    best0 = tops[0][0] + tops[1][0]
    st = {"sel": jnp.zeros(cand.shape, _F32), "z": jnp.zeros((1, tb), _F32)}

    def pick2(k, m, hit):
        st["sel"] = jnp.where(hit, 1.0, st["sel"])
        st["z"] = st["z"] + jnp.exp(m - best0)

    _topk16(cand, pick2)
    sel = st["sel"]
    count = [jnp.sum(sel[:16], axis=0, keepdims=True)]
    count += [jnp.sum(sel[8 + 8 * a:16 + 8 * a], axis=0, keepdims=True) for a in range(1, 8)]
    count += [sel[72 + i:73 + i] for i in range(8)]
    h1 = jnp.zeros((nk, tb), _F32)
    for a in range(_TOPK):
        h1 = h1 + jnp.where(ranks[0] == float(a), count[a], 0.0)
    h1_ref[hh] = h1
    r2_ref[hh] = ranks[1].astype(r2_ref.dtype)
    e1_ref[hh] = jnp.exp(scores[0] - tops[0][0])
    e2_ref[hh] = (jnp.exp(scores[1] - tops[1][0]) / st["z"]).astype(e2_ref.dtype)


_ROUTE_HEADS_PER_STEP = 8


def _route_body(q_ref, sk_ref, h1_ref, e1_ref, r2_ref, e2_ref):
    for hh in range(h1_ref.shape[0]):
        _route_head(hh, q_ref, sk_ref, h1_ref, e1_ref, r2_ref, e2_ref)


def peer_route(q, subkeys, tb):
    t = q.shape[0]
    n2, nk, half = subkeys.shape
    heads = n2 // 2
    hp = _ROUTE_HEADS_PER_STEP
    out = jax.ShapeDtypeStruct((heads, nk, t), _F32)
    out_b = jax.ShapeDtypeStruct((heads, nk, t), _BF)
    ospec = pl.BlockSpec((hp, nk, tb), lambda i, p: (p, 0, i))
    return pl.pallas_call(
        _route_body,
        grid=(t // tb, heads // hp),
        in_specs=[
            pl.BlockSpec((tb, hp * 2 * half), lambda i, p: (i, p)),
            pl.BlockSpec((hp * 2, nk, half), lambda i, p: (p, 0, 0)),
        ],
        out_specs=[ospec, ospec, ospec, ospec],
        out_shape=[out, out, out_b, out_b],
        compiler_params=_params(("parallel", "arbitrary"), 40),
        name="peer_route",
    )(q, subkeys)


_EXPERT_CHUNK = 256


def _peer_body(hf_ref, u_ref, v_ref, h1_ref, e1_ref, r2_ref, e2_ref, x_ref, o_ref, *, heads, nk):
    j = pl.program_id(1)

    @pl.when(j == 0)
    def _():
        o_ref[...] = x_ref[...]

    te = u_ref.shape[0]
    ck = min(_EXPERT_CHUNK, te)
    hf = hf_ref[...]
    ats = [lax.dot_general(u_ref[c * ck:(c + 1) * ck, :], hf, _NT, preferred_element_type=_F32)
           for c in range(te // ck)]
    tm = hf.shape[0]
    pack = 16
    for c, at in enumerate(ats):
        pieces = []
        for s in range(ck // nk):
            i1 = j * (te // nk) + c * (ck // nk) + s
            g = None
            for p in range(heads):
                h1row = jnp.broadcast_to(h1_ref[p, pl.ds(i1, 1), :], (pack, tm)).astype(_BF)
                e1row = jnp.broadcast_to(e1_ref[p, pl.ds(i1, 1), :], (pack, tm)).astype(_BF)
                r2 = r2_ref[p].reshape(nk // pack, pack, tm)
                e2 = e2_ref[p].reshape(nk // pack, pack, tm)
                term = jnp.where(r2 < h1row[None], e1row[None] * e2, jnp.zeros((), _BF))
                g = term if g is None else g + term
            act = _gelu(at[s * nk:(s + 1) * nk, :]).astype(_BF).reshape(nk // pack, pack, tm)
            pieces.append((act * g).reshape(nk, tm))
        b = jnp.concatenate(pieces, axis=0)
        o_ref[...] += lax.dot_general(b, v_ref[c * ck:(c + 1) * ck, :], (((0,), (0,)), ((), ())),
                                      preferred_element_type=_F32)


def peer_dense(x, hf, u, v, layer, route, *, tm, te, row_block0, n_row_blocks):
    t, d = hf.shape
    n_exp = u.shape[1]
    heads, nk, _ = route[0].shape
    once = pl.Buffered(1)
    rspec = pl.BlockSpec((heads, nk, tm), lambda i, j: (0, 0, row_block0 + i))
    in_specs = [
        pl.BlockSpec((tm, d), lambda i, j: (row_block0 + i, 0)),
        pl.BlockSpec((None, te, d), lambda i, j: (layer, j, 0)),
        pl.BlockSpec((None, te, d), lambda i, j: (layer, j, 0)),
        rspec, rspec, rspec, rspec,
        pl.BlockSpec((tm, d), lambda i, j: (row_block0 + i, 0), pipeline_mode=once),
    ]
    args = [hf, u, v, *route, x]
    return pl.pallas_call(
        functools.partial(_peer_body, heads=heads, nk=nk),
        grid=(n_row_blocks, n_exp // te),
        in_specs=in_specs,
        out_specs=pl.BlockSpec((tm, d), lambda i, j: (row_block0 + i, 0), pipeline_mode=once),
        out_shape=jax.ShapeDtypeStruct((t, d), _F32),
        input_output_aliases={len(args) - 1: 0},
        compiler_params=_params(("parallel", "arbitrary"), 58),
        name="peer_dense",
    )(*args)


def peer_ffn(x, g_ffn, wq, subkeys, u_b, v_b, layer, *, n_prompt, tiles):
    t, d = x.shape
    heads, _, nk, half = subkeys.shape
    hf = rmsnorm(x, g_ffn, tiles["norm"])
    q = matmul([hf], [wq.astype(_BF)], tm=tiles["mm"], tn=512, out_dtype=_BF, name="peer_q")
    route = peer_route(q, subkeys.reshape(heads * 2, nk, half).astype(_BF), 128)
    tm = tiles["peer"]
    ts = t - n_prompt
    te = tiles["experts"]
    y = peer_dense(x, hf, u_b, v_b, layer, route, tm=tm, te=te, row_block0=0, n_row_blocks=n_prompt // tm)
    return peer_dense(y, hf, u_b, v_b, layer, route, tm=ts, te=te, row_block0=n_prompt // ts, n_row_blocks=1)


def _kv_body(ckr_ref, cs_ref, gc_ref, gkn_ref, gkr_ref, wuk_ref, wuv_ref,
             c_ref, krope_ref, ks_ref, kn_ref, v_ref, *, heads, qk_head):
    kv_lora = c_ref.shape[1]
    c = _rms(ckr_ref[:, :kv_lora], gc_ref[...])
    c_ref[...] = c
    cb = c.astype(_BF)
    blk = ckr_ref[:, kv_lora:]
    rope = blk.shape[1] // 2
    lane = lax.broadcasted_iota(jnp.int32, blk.shape, 1)
    ssq_r = jnp.sum(jnp.where(lane < rope, blk * blk, 0.0), axis=-1, keepdims=True)
    t = blk * gkr_ref[...] * cs_ref[...]
    krope_ref[...] = t[:, :rope] + t[:, rope:]
    kn = jnp.dot(cb, wuk_ref[...], preferred_element_type=_F32)
    nope = kn.shape[1] // heads
    lane_h = lax.broadcasted_iota(jnp.int32, (kn.shape[0], 128), 1)
    ks = jnp.zeros((kn.shape[0], 128), _F32)
    for h in range(heads):
        x = kn[:, h * nope:(h + 1) * nope]
        ms = (jnp.sum(x * x, axis=-1, keepdims=True) + ssq_r) / qk_head
        ks = jnp.where(lane_h == h, lax.rsqrt(ms + EPS), ks)
        kn_ref[:, h * nope:(h + 1) * nope] = (x * gkn_ref[...]).astype(kn_ref.dtype)
    ks_ref[...] = ks[:, :heads]
    v_ref[...] = jnp.dot(cb, wuv_ref[...], preferred_element_type=_F32).astype(v_ref.dtype)


def latent_kv(ckr, cs_tab, g_c, g_k, w_uk, w_uv, *, heads, tb):
    t, w = ckr.shape
    kv_lora = w_uk.shape[0]
    rope = (w - kv_lora) // 2
    nope = w_uk.shape[1] // heads
    g_kr = g_k[nope:]
    gkr2 = jnp.concatenate([g_kr, jnp.concatenate([g_kr[rope // 2:], g_kr[:rope // 2]])]).reshape(1, 2 * rope)
    full = lambda i: (0, 0)
    row = lambda i: (i, 0)
    return pl.pallas_call(
        functools.partial(_kv_body, heads=heads, qk_head=float(nope + rope)),
        grid=(t // tb,),
        in_specs=[
            pl.BlockSpec((tb, w), row),
            pl.BlockSpec((tb, 2 * rope), row),
            pl.BlockSpec((1, kv_lora), full),
            pl.BlockSpec((1, nope), full),
            pl.BlockSpec((1, 2 * rope), full),
            pl.BlockSpec(w_uk.shape, full),
            pl.BlockSpec(w_uv.shape, full),
        ],
        out_specs=[
            pl.BlockSpec((tb, kv_lora), row),
            pl.BlockSpec((tb, rope), row),
            pl.BlockSpec((tb, heads), row),
            pl.BlockSpec((tb, w_uk.shape[1]), row),
            pl.BlockSpec((tb, w_uv.shape[1]), row),
        ],
        out_shape=[
            jax.ShapeDtypeStruct((t, kv_lora), _F32),
            jax.ShapeDtypeStruct((t, rope), _F32),
            jax.ShapeDtypeStruct((t, heads), _F32),
            jax.ShapeDtypeStruct((t, w_uk.shape[1]), _BF),
            jax.ShapeDtypeStruct((t, w_uv.shape[1]), _BF),
        ],
        compiler_params=_params(("parallel",), 48),
        name="latent_kv",
    )(ckr, cs_tab, g_c.reshape(1, -1), g_k[:nope].reshape(1, -1), gkr2, w_uk, w_uv)


def _q_body(q_ref, c2_ref, s2_ref, gn_ref, gr_ref, grs_ref, qn_ref, qr_ref, *, heads, nope, rope):
    n_all = heads * nope
    r_all = heads * rope
    qk_head = float(nope + rope)
    lane = lax.broadcasted_iota(jnp.int32, (q_ref.shape[0], 2 * rope), 1)
    first = lane < rope
    for j in range(heads // 2):
        n0 = q_ref[:, (2 * j) * nope:(2 * j + 1) * nope]
        n1 = q_ref[:, (2 * j + 1) * nope:(2 * j + 2) * nope]
        r = q_ref[:, n_all + j * 2 * rope:n_all + (j + 1) * 2 * rope]
        rs = q_ref[:, n_all + r_all + j * 2 * rope:n_all + r_all + (j + 1) * 2 * rope]
        r2 = r * r
        ss0 = jnp.sum(n0 * n0, axis=-1, keepdims=True) + jnp.sum(jnp.where(first, r2, 0.0), axis=-1, keepdims=True)
        ss1 = jnp.sum(n1 * n1, axis=-1, keepdims=True) + jnp.sum(jnp.where(first, 0.0, r2), axis=-1, keepdims=True)
        inv0 = lax.rsqrt(ss0 / qk_head + EPS)
        inv1 = lax.rsqrt(ss1 / qk_head + EPS)
        qn_ref[:, (2 * j) * nope:(2 * j + 1) * nope] = n0 * inv0 * gn_ref[...]
        qn_ref[:, (2 * j + 1) * nope:(2 * j + 2) * nope] = n1 * inv1 * gn_ref[...]
        inv_r = jnp.where(first, inv0, inv1)
        qr_ref[:, j * 2 * rope:(j + 1) * 2 * rope] = (
            r * inv_r * gr_ref[...] * c2_ref[...] + rs * inv_r * grs_ref[...] * s2_ref[...])


def mla_q_finish(qraw, c2_tab, s2_tab, g_q, *, heads, nope, rope, tb):
    t, w = qraw.shape
    g_r = g_q[nope:]
    g_rs = jnp.concatenate([g_r[rope // 2:], g_r[:rope // 2]])
    row = lambda i: (i, 0)
    full = lambda i: (0, 0)
    return pl.pallas_call(
        functools.partial(_q_body, heads=heads, nope=nope, rope=rope),
        grid=(t // tb,),
        in_specs=[
            pl.BlockSpec((tb, w), row),
            pl.BlockSpec((tb, 2 * rope), row),
            pl.BlockSpec((tb, 2 * rope), row),
            pl.BlockSpec((1, nope), full),
            pl.BlockSpec((1, 2 * rope), full),
            pl.BlockSpec((1, 2 * rope), full),
        ],
        out_specs=[pl.BlockSpec((tb, heads * nope), row), pl.BlockSpec((tb, heads * rope), row)],
        out_shape=[jax.ShapeDtypeStruct((t, heads * nope), _F32), jax.ShapeDtypeStruct((t, heads * rope), _F32)],
        compiler_params=_params(("parallel",), 48),
        name="mla_q_finish",
    )(qraw, c2_tab, s2_tab, g_q[:nope].reshape(1, -1),
      jnp.tile(g_r, 2).reshape(1, -1), jnp.tile(g_rs, 2).reshape(1, -1))


def _flash_body(qn_ref, qr_ref, kn_ref, kr_ref, ks_ref, v_ref, o_ref, *, nope, rope, n_q_tiles):
    tq = qn_ref.shape[0]
    qi = pl.program_id(2)
    causal = (lax.broadcasted_iota(jnp.int32, (tq, tq), 1) <= lax.broadcasted_iota(jnp.int32, (tq, tq), 0))

    def tile(lo):
        for hh in range(2):
            hs = slice(hh * nope, (hh + 1) * nope)
            q = jnp.concatenate([qn_ref[:, hs], qr_ref[:, hh * rope:(hh + 1) * rope]], axis=1).astype(_BF)

            def scores(k0, k1, hh=hh, hs=hs, q=q):
                k = jnp.concatenate([kn_ref[k0:k1, hs], kr_ref[k0:k1, :]], axis=1)
                s = lax.dot_general(q, k, _NT, preferred_element_type=_F32)
                return s * ks_ref[0, hh:hh + 1, k0:k1]

            s_d = jnp.where(causal, scores(lo, lo + tq), -jnp.inf)
            m = jnp.max(s_d, axis=-1, keepdims=True)
            if lo:
                s_f = scores(0, lo)
                m = jnp.maximum(m, jnp.max(s_f, axis=-1, keepdims=True))
            p = jnp.exp2(s_d - m)
            l = jnp.sum(p, axis=-1, keepdims=True)
            acc = jnp.dot(p.astype(_BF), v_ref[lo:lo + tq, hs], preferred_element_type=_F32)
            if lo:
                p = jnp.exp2(s_f - m)
                l = l + jnp.sum(p, axis=-1, keepdims=True)
                acc = acc + jnp.dot(p.astype(_BF), v_ref[0:lo, hs], preferred_element_type=_F32)
            o_ref[:, hs] = (acc / l).astype(o_ref.dtype)

    for t in range(n_q_tiles):
        pl.when(qi == t)(functools.partial(tile, t * tq))


def mla_prompt_attention(qn, qr, kn, krope, ks_t, v, *, n_batch, seq, heads, nope, rope, tq):
    nq = seq // tq
    hp = heads // 2
    return pl.pallas_call(
        functools.partial(_flash_body, nope=nope, rope=rope, n_q_tiles=nq),
        grid=(n_batch, hp, nq),
        in_specs=[
            pl.BlockSpec((tq, 2 * nope), lambda n, h, i: (n * nq + i, h)),
            pl.BlockSpec((tq, 2 * rope), lambda n, h, i: (n * nq + i, h)),
            pl.BlockSpec((seq, 2 * nope), lambda n, h, i: (n, h)),
            pl.BlockSpec((seq, rope), lambda n, h, i: (n, 0)),
            pl.BlockSpec((1, 2, seq), lambda n, h, i: (n * hp + h, 0, 0)),
            pl.BlockSpec((seq, 2 * nope), lambda n, h, i: (n, h)),
        ],
        out_specs=pl.BlockSpec((tq, 2 * nope), lambda n, h, i: (n * nq + i, h)),
        out_shape=jax.ShapeDtypeStruct((n_batch * seq, heads * nope), _BF),
        compiler_params=_params(("parallel", "parallel", "arbitrary"), 40),
        name="mla_prompt_attention",
    )(qn, qr, kn, krope, ks_t, v)


def _head_nt_body(x_ref, g_ref, w_ref, o_ref):
    x = (x_ref[...] * g_ref[...]).astype(_BF)
    o_ref[...] = lax.dot_general(x, w_ref[...], _NT, preferred_element_type=_F32)


def absorb_q(qn_s, g_kn, w_uk):
    n = qn_s.shape[0]
    nope = g_kn.shape[0]
    kv_lora, w = w_uk.shape
    heads = w // nope
    return pl.pallas_call(
        _head_nt_body,
        grid=(heads,),
        in_specs=[
            pl.BlockSpec((n, nope), lambda h: (0, h)),
            pl.BlockSpec((1, nope), lambda h: (0, 0)),
            pl.BlockSpec((kv_lora, nope), lambda h: (0, h)),
        ],
        out_specs=pl.BlockSpec((n, kv_lora), lambda h: (0, h)),
        out_shape=jax.ShapeDtypeStruct((n, heads * kv_lora), _F32),
        compiler_params=_params(("parallel",), 40),
        name="absorb_q",
    )(qn_s, g_kn.reshape(1, nope), w_uk)


def _head_nn_body(x_ref, w_ref, o_ref):
    o_ref[...] = jnp.dot(x_ref[...].astype(_BF), w_ref[...], preferred_element_type=_F32).astype(o_ref.dtype)


def project_latent_out(o_lat, w_uv, heads):
    n = o_lat.shape[0]
    kv_lora = w_uv.shape[0]
    vh = w_uv.shape[1] // heads
    return pl.pallas_call(
        _head_nn_body,
        grid=(heads,),
        in_specs=[
            pl.BlockSpec((n, kv_lora), lambda h: (0, h)),
            pl.BlockSpec((kv_lora, vh), lambda h: (0, h)),
        ],
        out_specs=pl.BlockSpec((n, vh), lambda h: (0, h)),
        out_shape=jax.ShapeDtypeStruct((n, heads * vh), _BF),
        compiler_params=_params(("parallel",), 40),
        name="project_latent_out",
    )(o_lat, w_uv)


_PAGE_RING_SLOTS = 4

def _paged_body(pt_ref, qlat_ref, qr_ref, cnew_ref, krnew_ref, ksnew_ref, ckv_hbm, kr_hbm, ks_hbm,
                o_ref, c_buf, kr_buf, ks_buf, sem, *, pages_per_group, n_groups, scale):
    n = pl.program_id(0)
    n_samples = pl.num_programs(0)
    heads = qlat_ref.shape[1]
    slots = c_buf.shape[0]
    ahead = slots - 1
    page = c_buf.shape[2]
    keys = pages_per_group * page

    def group_copies(sample, g, slot):
        cps = []
        for p in range(pages_per_group):
            pg = pt_ref[sample, g * pages_per_group + p]
            cps.append(pltpu.make_async_copy(ckv_hbm.at[pg], c_buf.at[slot, p], sem.at[slot, 0]))
            cps.append(pltpu.make_async_copy(kr_hbm.at[pg], kr_buf.at[slot, p], sem.at[slot, 1]))
            cps.append(pltpu.make_async_copy(ks_hbm.at[pg], ks_buf.at[slot, p], sem.at[slot, 2]))
        return cps

    def start_group(w):
        sample = w // n_groups

        @pl.when(sample < n_samples)
        def _():
            for cp in group_copies(sample, w % n_groups, w % slots):
                cp.start()

    @pl.when(n == 0)
    def _():
        for w in range(ahead):
            start_group(w)

    qlat32 = qlat_ref[0]
    qr32 = qr_ref[0]
    qlat = qlat32.astype(_BF)
    qr = qr32.astype(_BF)
    c_new = cnew_ref[0]
    s0 = (jnp.sum(qlat32 * c_new, axis=-1, keepdims=True)
          + jnp.sum(qr32 * krnew_ref[0], axis=-1, keepdims=True)) * ksnew_ref[0] * scale
    init = (s0, jnp.ones((heads, 1), _F32), jnp.broadcast_to(c_new, qlat32.shape))

    def group(g, carry):
        m, l, acc = carry
        w = n * n_groups + g
        slot = w % slots
        start_group(w + ahead)

        for k, buf in enumerate((c_buf, kr_buf, ks_buf)):
            pltpu.make_async_copy(buf.at[slot], buf.at[slot], sem.at[slot, k]).wait()
        c = c_buf[slot].reshape(keys, c_buf.shape[3]).astype(_BF)
        pages = range(pages_per_group)
        s_r = jnp.concatenate([jnp.dot(qr, kr_buf[slot, p].astype(_BF), preferred_element_type=_F32)
                               for p in pages], axis=1)
        s = lax.dot_general(qlat, c, _NT, preferred_element_type=_F32) + s_r
        ks_t = jnp.concatenate([ks_buf[slot, p] for p in pages], axis=1)
        s = s * (ks_t * scale)
        m_new = jnp.maximum(m, jnp.max(s, axis=-1, keepdims=True))
        alpha = jnp.exp(m - m_new)
        p_exp = jnp.exp(s - m_new)
        l = l * alpha + jnp.sum(p_exp, axis=-1, keepdims=True)
        acc = acc * alpha + jnp.dot(p_exp.astype(_BF), c, preferred_element_type=_F32)
        return m_new, l, acc

    m, l, acc = lax.fori_loop(0, n_groups, group, init)
    o_ref[0] = acc / l


def paged_latent_attention(q_lat, q_r, c_new, kr_new, ks_new, cache_ckv, cache_krope_t, cache_kscale_t,
                           page_table, scale, pages_per_group):
    n, heads, kv_lora = q_lat.shape
    rope = q_r.shape[2]
    page = cache_ckv.shape[1]
    n_pages = page_table.shape[1]
    n_groups = n_pages // pages_per_group
    assert n_groups * pages_per_group == n_pages
    slots = _PAGE_RING_SLOTS
    per = lambda i, pt: (i, 0, 0)
    grid_spec = pltpu.PrefetchScalarGridSpec(
        num_scalar_prefetch=1,
        grid=(n,),
        in_specs=[
            pl.BlockSpec((1, heads, kv_lora), per),
            pl.BlockSpec((1, heads, rope), per),
            pl.BlockSpec((1, 1, kv_lora), per),
            pl.BlockSpec((1, 1, rope), per),
            pl.BlockSpec((1, heads, 1), per),
            pl.BlockSpec(memory_space=pl.ANY),
            pl.BlockSpec(memory_space=pl.ANY),
            pl.BlockSpec(memory_space=pl.ANY),
        ],
        out_specs=pl.BlockSpec((1, heads, kv_lora), per),
        scratch_shapes=[
            pltpu.VMEM((slots, pages_per_group, page, kv_lora), _F32),
            pltpu.VMEM((slots, pages_per_group, rope, page), _F32),
            pltpu.VMEM((slots, pages_per_group, heads, page), _F32),
            pltpu.SemaphoreType.DMA((slots, 3)),
        ],
    )
    return pl.pallas_call(
        functools.partial(_paged_body, pages_per_group=pages_per_group, n_groups=n_groups, scale=scale),
        grid_spec=grid_spec,
        out_shape=jax.ShapeDtypeStruct((n, heads, kv_lora), _F32),
        compiler_params=_params(("arbitrary",), 40),
        name="paged_latent_attention",
    )(page_table, q_lat, q_r, c_new, kr_new, ks_new, cache_ckv, cache_krope_t, cache_kscale_t)


def _rope_tables(pos, rope):
    inv = 1.0 / (ROPE_THETA ** (jnp.arange(0, rope, 2, dtype=_F32) / rope))
    ang = pos.astype(_F32)[:, None] * inv[None, :]
    return jnp.cos(ang), jnp.sin(ang)


def kernel(x_prompt, x_sample, cache_ckv, cache_krope, cache_kscale, cache_mem_k, cache_mem_v, page_table, mem_prompt, g_mix, g_ffn, g_mem_in, w_mem_k, w_mem_v, g_mem_q, g_mem_k, w_out, peer_wq, peer_subkeys, peer_u, peer_v, a_w_in, a_g_v, a_w_s, a_b_s, b_w_in, b_g_cq, b_w_qb, b_g_q, kv_g_in, kv_w_dkv, kv_w_kr, kv_g_c, kv_g_k, kv_w_uk, kv_w_uv):
    n_batch, seq, d = x_prompt.shape
    n_dec = x_sample.shape[0]
    assert x_sample.shape[1] == 1
    depth = g_mix.shape[0]
    n_a = a_w_in.shape[0]
    mem_tokens, mem_heads, mem_hd = cache_mem_k.shape[2:]
    mem_w = mem_heads * mem_hd
    a_w = a_g_v.shape[1]
    groups, chunk = a_w_s.shape[1], a_w_s.shape[2]
    kv_lora, heads, nope = kv_w_uk.shape
    assert kv_w_uv.shape[2] == nope
    rope = kv_w_kr.shape[1]
    q_lora = b_g_cq.shape[1]
    n_p = n_batch * seq
    t = n_p + n_dec
    n_pages = page_table.shape[1]
    past = n_pages * cache_ckv.shape[1]
    tiles = {
        "norm": _tile(t, (416, 320, 128)),
        "mm": _tile(t, (832, 640, 128)),
        "peer": _tile(n_p, (512, 256, 128)),
        "experts": 512,
        "seq": _tile(seq, (512, 256, 128)),
        "pages": _tile(n_pages, (16, 8, 4, 2, 1)),
    }

    x = jnp.concatenate([x_prompt.reshape(n_p, d), x_sample.reshape(n_dec, d)], axis=0)
    mem = mem_prompt.reshape(n_batch * mem_tokens, d)

    cos_p, sin_p = _rope_tables(jnp.arange(seq), rope)
    cos_s, sin_s = _rope_tables(past + jnp.arange(1), rope)
    cos = jnp.concatenate([jnp.tile(cos_p, (n_batch, 1)), jnp.tile(cos_s, (n_dec, 1))], axis=0)
    sin = jnp.concatenate([jnp.tile(sin_p, (n_batch, 1)), jnp.tile(sin_s, (n_dec, 1))], axis=0)
    cos2 = jnp.concatenate([cos, cos], axis=1)
    sin2 = jnp.concatenate([-sin, sin], axis=1)

    peer_u_b = peer_u.astype(_BF)
    peer_v_b = peer_v.astype(_BF)
    mem_k_new, mem_v_new, chunk_v_new = [], [], []
    kv_out = None
    for l in range(depth):
        hm = rmsnorm(mem, g_mem_in[l], _tile(mem.shape[0], (256, 128)))
        tmm = _tile(mem.shape[0], (512, 256, 128))
        mk = matmul([hm], [w_mem_k[l].astype(_BF)], tm=tmm, tn=512, out_dtype=_F32,
                    extras=[g_mem_k[l].reshape(1, mem_hd)],
                    epilogue=functools.partial(_headnorm_epilogue, width=mem_hd), name="mem_k")
        mv = matmul([hm], [w_mem_v[l].astype(_BF)], tm=tmm, tn=512, out_dtype=_F32, name="mem_v")
        mem_k_new.append(mk.reshape(n_batch, mem_tokens, mem_heads, mem_hd))
        mem_v_new.append(mv.reshape(n_batch, mem_tokens, mem_heads, mem_hd))

        if l == n_a:
            hk = rmsnorm(x, kv_g_in, tiles["norm"])
            kr_sw = jnp.concatenate([kv_w_kr[:, rope // 2:], kv_w_kr[:, :rope // 2]], axis=1)
            w_ckr = jnp.concatenate([kv_w_dkv, kv_w_kr, kr_sw], axis=1).astype(_BF)
            ckr = matmul([hk], [w_ckr], tm=tiles["mm"], tn=w_ckr.shape[1], out_dtype=_F32, name="kv_down")
            w_uk2 = kv_w_uk.reshape(kv_lora, heads * nope).astype(_BF)
            w_uv2 = kv_w_uv.reshape(kv_lora, -1).astype(_BF)
            c_all, krope_all, ks_all, kn_all, v_all = latent_kv(
                ckr, jnp.concatenate([cos2, sin2], axis=1), kv_g_c, kv_g_k, w_uk2, w_uv2,
                heads=heads, tb=tiles["norm"])
            kv_out = (c_all, krope_all, ks_all)

        h = rmsnorm(x, g_mix[l], tiles["norm"])
        if l < n_a:
            z = matmul([h], [a_w_in[l].astype(_BF)], tm=tiles["mm"], tn=512, out_dtype=_F32, name="a_in")
            tril = jnp.tril(a_w_s[l])
            diag = a_w_s[l][:, 0, 0][:, None, None] * jnp.eye(chunk, dtype=_F32)[None]
            ws_eff = jnp.stack([tril, diag])
            b_eff = jnp.stack([a_b_s[l].T, jnp.broadcast_to(a_b_s[l][:, 0][None, :], (chunk, groups))])
            mix, v_s = gating(z, a_g_v[l], ws_eff, b_eff, a_w, n_p // chunk)
            chunk_v_new.append(v_s.reshape(n_dec, 1, a_w))
            qm_col = 2 * a_w // mem_w
        else:
            b = l - n_a
            z = matmul([h], [b_w_in[b].astype(_BF)], tm=tiles["mm"], tn=512, out_dtype=_F32, name="b_in")
            cq = rmsnorm(z, b_g_cq[b], tiles["norm"])
            w3 = b_w_qb[b].reshape(q_lora, heads, nope + rope)
            w_r = w3[:, :, nope:]
            w_rs = jnp.concatenate([w_r[:, :, rope // 2:], w_r[:, :, :rope // 2]], axis=2)
            w_q = jnp.concatenate([w3[:, :, :nope].reshape(q_lora, heads * nope),
                                   w_r.reshape(q_lora, heads * rope),
                                   w_rs.reshape(q_lora, heads * rope)], axis=1).astype(_BF)
            qraw = matmul([cq], [w_q], tm=tiles["mm"], tn=512, out_dtype=_F32, name="q_up")
            qn, qr = mla_q_finish(qraw, jnp.tile(cos2, (1, 2)), jnp.tile(sin2, (1, 2)), b_g_q[b],
                                  heads=heads, nope=nope, rope=rope, tb=tiles["norm"])
            scale = float(nope + rope) ** -0.5
            ks_t = (ks_all[:n_p] * (scale * _LOG2E)).reshape(n_batch, seq, heads // 2, 2).transpose(0, 2, 3, 1)
            ks_t = ks_t.reshape(n_batch * (heads // 2), 2, seq)
            tq = tiles["seq"]
            mix_p = mla_prompt_attention(qn, qr, kn_all, krope_all.astype(_BF), ks_t, v_all,
                                         n_batch=n_batch, seq=seq, heads=heads, nope=nope, rope=rope,
                                         tq=tq)
            q_lat = absorb_q(qn[n_p:], kv_g_k[:nope], w_uk2)
            o_lat = paged_latent_attention(
                q_lat.reshape(n_dec, heads, kv_lora), qr[n_p:].reshape(n_dec, heads, rope),
                c_all[n_p:].reshape(n_dec, 1, kv_lora), krope_all[n_p:].reshape(n_dec, 1, rope),
                ks_all[n_p:].reshape(n_dec, heads, 1), cache_ckv,
                jnp.transpose(cache_krope, (0, 2, 1)), jnp.transpose(cache_kscale, (0, 2, 1)),
                page_table, scale, tiles["pages"])
            mix_s = project_latent_out(o_lat.reshape(n_dec, heads * kv_lora), w_uv2, heads)
            mix = jnp.concatenate([mix_p, mix_s], axis=0)
            qm_col = q_lora // mem_w

        mo_p = memattn_prompt(z, qm_col, mk.reshape(n_batch, mem_tokens, mem_w), mv.reshape(n_batch, mem_tokens, mem_w),
                              g_mem_q[l], n_batch=n_batch, seq=seq, tq=tiles["seq"], heads=mem_heads)
        qm_s = z[n_p:, qm_col * mem_w:(qm_col + 1) * mem_w].reshape(n_dec, mem_heads, mem_hd)
        mo_s = memattn_sample(qm_s, cache_mem_k, cache_mem_v, l, g_mem_q[l]).reshape(n_dec, mem_w)
        mo = jnp.concatenate([mo_p, mo_s.astype(_BF)], axis=0)
        w_o = w_out[l].astype(_BF)
        mix_w = mix.shape[1]
        x = matmul([mix, mo], [w_o[:mix_w], w_o[mix_w:]], tm=tiles["mm"], tn=512, out_dtype=_F32, res=x, name="w_out")
        x = peer_ffn(x, g_ffn[l], peer_wq[l], peer_subkeys[l], peer_u_b, peer_v_b, l, n_prompt=n_p, tiles=tiles)

    c_all, krope_all, ks_all = kv_out
    return (x[:n_p].reshape(n_batch, seq, d), x[n_p:].reshape(n_dec, 1, d),
            c_all[:n_p].reshape(n_batch, seq, kv_lora), krope_all[:n_p].reshape(n_batch, seq, rope),
            ks_all[:n_p].reshape(n_batch, seq, heads),
            jnp.stack(mem_k_new), jnp.stack(mem_v_new),
            c_all[n_p:].reshape(n_dec, 1, kv_lora), krope_all[n_p:].reshape(n_dec, 1, rope),
            ks_all[n_p:].reshape(n_dec, 1, heads),
            jnp.stack(chunk_v_new))
```
